```python
import math
import jax, jax.numpy as jnp
from jax import lax
import numpy as np

D_MODEL = 1024
BATCH = 8
SEQ = 4096
DEPTH = 1

N_META = 16
EPS = 1e-6

D_HY = 512
HY_GROUPS = 8
HY_SHORT = 3
HY_EMB = 33
HY_BANDS = (HY_EMB - 1) // 2
HY_FFN = 64
HY_TARGET = 1e-2
HY_DECAY_HI = 0.3
HY_DECAY_LO = 1.5

D_RG = 1024
RG_HEADS = 8
RG_HEAD_DIM = D_RG // RG_HEADS
RG_CONV = 4
RG_C = 8.0

PEER_HEADS = 8
N_KEYS = 128
N_EXPERTS = N_KEYS * N_KEYS
D_KEY = 256
D_HALF = D_KEY // 2
PEER_TOPK = 16
PEER_CHUNK = 16

D_IN = 3 * D_HY + 2 * D_RG + 2 * D_MODEL

kernel_name = 'hybrid_hyena_rglru_peer_encoder'


def rmsnorm(x, g):
    xf = x.astype(jnp.float32)
    y = xf * lax.rsqrt(jnp.mean(xf * xf, axis=-1, keepdims=True) + EPS)
    return (y * g.astype(jnp.float32)).astype(x.dtype)


def hyena_filter(L, w1, b1, w2, b2, w3, b3, freq):
    f32 = jnp.float32
    t = jnp.linspace(0.0, 1.0, L, dtype=f32)[:, None]
    w = (2.0 * math.pi / L) * jnp.arange(L, dtype=f32)[:, None]
    bands = jnp.linspace(1e-4, HY_BANDS - 1, HY_BANDS, dtype=f32)[None, :]
    z = jnp.concatenate([t, jnp.cos(bands * w), -jnp.sin(bands * w)], axis=-1)
    freq = freq.astype(f32)
    hdn = jnp.sin(freq[0] * (z @ w1.astype(f32) + b1.astype(f32)))
    hdn = jnp.sin(freq[1] * (hdn @ w2.astype(f32) + b2.astype(f32)))
    k = (hdn @ w3.astype(f32) + b3.astype(f32)).reshape(L, 2, D_HY)
    deltas = jnp.abs(jnp.linspace(math.log(HY_TARGET) / HY_DECAY_HI,
                                  math.log(HY_TARGET) / HY_DECAY_LO, D_HY, dtype=f32))
    k = k * jnp.exp(-t * deltas)[:, None, :]
    k_fwd, k_bwd = k[:, 0], k[:, 1]
    return jnp.concatenate([k_fwd, jnp.zeros((1, D_HY), f32), k_bwd[:0:-1]], axis=0)


def fft_long_conv(u, k_full, skip):
    L = u.shape[1]
    uf = u.astype(jnp.float32)
    U = jnp.fft.rfft(uf, n=2 * L, axis=1)
    K = jnp.fft.rfft(k_full, axis=0)
    y = jnp.fft.irfft(U * K[None], n=2 * L, axis=1)[:, :L]
    return (y + uf * skip.astype(jnp.float32)).astype(u.dtype)


def hyena_branch(hy_in, conv_w, conv_b, w1, b1, w2, b2, w3, b3, freq, skip, w_proj):
    B, L, _ = hy_in.shape
    hp = jnp.pad(hy_in, ((0, 0), (1, 1), (0, 0)))
    u = hp[:, 0:L] * conv_w[0] + hp[:, 1:L + 1] * conv_w[1] + hp[:, 2:L + 2] * conv_w[2] + conv_b
    x0, x1, v = u[..., :D_HY], u[..., D_HY:2 * D_HY], u[..., 2 * D_HY:]
    k_full = hyena_filter(L, w1, b1, w2, b2, w3, b3, freq)
    y = fft_long_conv(v * x1, k_full, skip) * x0
    return y @ w_proj


def rglru_direction(x, conv_w, conv_b, gate_w, gate_b, lam):
    B, L, _ = x.shape
    xp = jnp.pad(x, ((0, 0), (RG_CONV - 1, 0), (0, 0)))
    xc = conv_b + sum(xp[:, RG_CONV - 1 - j:RG_CONV - 1 - j + L] * conv_w[j] for j in range(RG_CONV))
    xh = xc.reshape(B, L, RG_HEADS, RG_HEAD_DIM)
    gates = jnp.einsum('blhi,ghij->gblhj', xh, gate_w).reshape(2, B, L, D_RG)
    gates = gates.astype(jnp.float32) + gate_b.astype(jnp.float32)[:, None, None, :]
    r = jax.nn.sigmoid(gates[0])
    i = jax.nn.sigmoid(gates[1])
    log_a = -RG_C * r * jax.nn.softplus(-lam.astype(jnp.float32))
    a = jnp.exp(log_a)
    mult = jnp.sqrt(-jnp.expm1(2.0 * log_a))
    mult = mult.at[:, 0].set(1.0)
    b = mult * i * xc.astype(jnp.float32)

    def combine(p, q):
        a1, b1 = p
        a2, b2 = q
        return a1 * a2, a2 * b1 + b2

    _, h = lax.associative_scan(combine, (a, b), axis=1)
    return h.astype(x.dtype)


def rglru_branch(rg_x, rg_g, conv_w, conv_b, gate_w, gate_b, lam, w_proj):
    h_f = rglru_direction(rg_x, conv_w[0], conv_b[0], gate_w[0], gate_b[0], lam[0])
    h_b = rglru_direction(rg_x[:, ::-1], conv_w[1], conv_b[1], gate_w[1], gate_b[1], lam[1])[:, ::-1]
    y = (h_f + h_b) * jax.nn.gelu(rg_g)
    return y @ w_proj


def peer(x, wq, keys, u_tab, v_tab):
    B, L, D = x.shape
    K = PEER_TOPK
    q = (x @ wq).reshape(B, L, PEER_HEADS, 2, D_HALF)
    s = jnp.einsum('blhpd,hpkd->blhpk', q, keys).astype(jnp.float32)
    ts, ti = lax.top_k(s, K)
    cand = ts[..., 0, :, None] + ts[..., 1, None, :]
    cs, ci = lax.top_k(cand.reshape(B, L, PEER_HEADS, K * K), K)
    e1 = jnp.take_along_axis(ti[..., 0, :], ci // K, axis=-1)
    e2 = jnp.take_along_axis(ti[..., 1, :], ci % K, axis=-1)
    experts = e1 * N_KEYS + e2
    g = jax.nn.softmax(cs, axis=-1).astype(x.dtype)

    nc = L // PEER_CHUNK
    hk = PEER_HEADS * K

    def to_chunks(a, last):
        return a.reshape(B, nc, PEER_CHUNK, last).transpose(1, 0, 2, 3).reshape(nc, B * PEER_CHUNK, last)

    xs = (to_chunks(x, D), to_chunks(experts.reshape(B, L, hk), hk), to_chunks(g.reshape(B, L, hk), hk))

    def chunk_fn(args):
        xt, et, gt = args
        act = jax.nn.gelu(jnp.einsum('td,tkd->tk', xt, u_tab[et])) * gt
        return jnp.einsum('tk,tkd->td', act, v_tab[et])

    out = lax.map(chunk_fn, xs)
    return out.reshape(nc, B, PEER_CHUNK, D).transpose(1, 0, 2, 3).reshape(B, L, D)


def setup_inputs(seed: int = 0) -> dict:
    key = jax.random.key(seed)
    ks = iter(jax.random.split(key, 40))
    f32 = jnp.float32

    def nrm(shape, scale):
        return jax.random.normal(next(ks), shape, f32) * scale

    L_TOT = SEQ + N_META
    x = nrm((BATCH, SEQ, D_MODEL), 1.0)
    meta = nrm((N_META, D_MODEL), 1.0)
    g_final = 1.0 + nrm((D_MODEL,), 0.02)
    g_mix = 1.0 + nrm((DEPTH, D_MODEL), 0.02)
    w_in = nrm((DEPTH, D_MODEL, D_IN), D_MODEL ** -0.5)
    b_gate = nrm((DEPTH, 2 * D_MODEL), 0.02)
    hy_conv_w = nrm((DEPTH, HY_SHORT, 3 * D_HY), HY_SHORT ** -0.5)
    hy_conv_b = nrm((DEPTH, 3 * D_HY), 0.02)
    hy_w1 = nrm((DEPTH, HY_EMB, HY_FFN), HY_EMB ** -0.5)
    hy_b1 = nrm((DEPTH, HY_FFN), 0.1)
    hy_w2 = nrm((DEPTH, HY_FFN, HY_FFN), HY_FFN ** -0.5)
    hy_b2 = nrm((DEPTH, HY_FFN), 0.1)
    hy_w3 = nrm((DEPTH, HY_FFN, 2 * D_HY), (HY_FFN * L_TOT / 4.0) ** -0.5)
    hy_b3 = nrm((DEPTH, 2 * D_HY), L_TOT ** -0.5)
    hy_freq = 1.0 + nrm((DEPTH, 2, HY_FFN), 0.02)
    hy_skip = nrm((DEPTH, D_HY), 1.0)
    hy_out = nrm((DEPTH, D_HY, D_MODEL), D_HY ** -0.5)
    rg_conv_w = nrm((DEPTH, 2, RG_CONV, D_RG), RG_CONV ** -0.5)
    rg_conv_b = nrm((DEPTH, 2, D_RG), 0.02)
    rg_gate_w = nrm((DEPTH, 2, 2, RG_HEADS, RG_HEAD_DIM, RG_HEAD_DIM), RG_HEAD_DIM ** -0.5)
    rg_gate_b = nrm((DEPTH, 2, 2, D_RG), 0.02)
    a_c = jax.random.uniform(next(ks), (DEPTH, 2, D_RG), f32, minval=0.9, maxval=0.999)
    a_base = a_c ** (1.0 / RG_C)
    rg_lambda = jnp.log(a_base) - jnp.log1p(-a_base)
    rg_out = nrm((DEPTH, D_RG, D_MODEL), D_RG ** -0.5)
    w_out = nrm((DEPTH, D_MODEL, D_MODEL), D_MODEL ** -0.5)
    g_ffn = 1.0 + nrm((DEPTH, D_MODEL), 0.02)
    peer_wq = nrm((DEPTH, D_MODEL, PEER_HEADS * D_KEY), D_MODEL ** -0.5)
    peer_keys = nrm((DEPTH, PEER_HEADS, 2, N_KEYS, D_HALF), D_HALF ** -0.5)
    peer_u = nrm((DEPTH, N_EXPERTS, D_MODEL), D_MODEL ** -0.5)
    peer_v = nrm((DEPTH, N_EXPERTS, D_MODEL), PEER_HEADS ** -0.5)
    return {'x': x, 'meta': meta, 'g_final': g_final, 'g_mix': g_mix, 'w_in': w_in, 'b_gate': b_gate,
            'hy_conv_w': hy_conv_w, 'hy_conv_b': hy_conv_b, 'hy_w1': hy_w1, 'hy_b1': hy_b1,
            'hy_w2': hy_w2, 'hy_b2': hy_b2, 'hy_w3': hy_w3, 'hy_b3': hy_b3, 'hy_freq': hy_freq,
            'hy_skip': hy_skip, 'hy_out': hy_out, 'rg_conv_w': rg_conv_w, 'rg_conv_b': rg_conv_b,
            'rg_gate_w': rg_gate_w, 'rg_gate_b': rg_gate_b, 'rg_lambda': rg_lambda, 'rg_out': rg_out,
            'w_out': w_out, 'g_ffn': g_ffn, 'peer_wq': peer_wq, 'peer_keys': peer_keys,
            'peer_u': peer_u, 'peer_v': peer_v}


def reference(x, meta, g_final, g_mix, w_in, b_gate, hy_conv_w, hy_conv_b, hy_w1, hy_b1, hy_w2, hy_b2,
              hy_w3, hy_b3, hy_freq, hy_skip, hy_out, rg_conv_w, rg_conv_b, rg_gate_w, rg_gate_b,
              rg_lambda, rg_out, w_out, g_ffn, peer_wq, peer_keys, peer_u, peer_v):
    B = x.shape[0]
    h = jnp.concatenate([jnp.broadcast_to(meta.astype(x.dtype)[None], (B, N_META, D_MODEL)), x], axis=1)
    s_hy = 3 * D_HY
    s_rx = s_hy + D_RG
    s_rg = s_rx + D_RG
    for l in range(DEPTH):
        n = rmsnorm(h, g_mix[l])
        proj = n @ w_in[l]
        y_hy = hyena_branch(proj[..., :s_hy], hy_conv_w[l], hy_conv_b[l], hy_w1[l], hy_b1[l], hy_w2[l],
                            hy_b2[l], hy_w3[l], hy_b3[l], hy_freq[l], hy_skip[l], hy_out[l])
        y_rg = rglru_branch(proj[..., s_hy:s_rx], proj[..., s_rx:s_rg], rg_conv_w[l], rg_conv_b[l],
                            rg_gate_w[l], rg_gate_b[l], rg_lambda[l], rg_out[l])
        gl = proj[..., s_rg:] + b_gate[l]
        merged = jax.nn.sigmoid(gl[..., :D_MODEL]) * y_hy + jax.nn.sigmoid(gl[..., D_MODEL:]) * y_rg
        h = h + merged @ w_out[l]
        h = h + peer(rmsnorm(h, g_ffn[l]), peer_wq[l], peer_keys[l], peer_u[l], peer_v[l])
    h = rmsnorm(h, g_final)
    return h[:, N_META:]
```

```python
import functools
import math

import jax
import jax.numpy as jnp
from jax import lax
from jax.experimental import pallas as pl
from jax.experimental.pallas import tpu as pltpu

D_MODEL = 1024
N_META = 16
EPS = 1e-6
D_HY = 512
HY_EMB = 33
HY_BANDS = (HY_EMB - 1) // 2
HY_TARGET = 1e-2
HY_DECAY_HI = 0.3
HY_DECAY_LO = 1.5
D_RG = 1024
RG_HEADS = 8
RG_HEAD_DIM = D_RG // RG_HEADS
RG_CONV = 4
RG_C = 8.0
PEER_HEADS = 8
N_KEYS = 128
D_KEY = 256
D_HALF = D_KEY // 2
PEER_TOPK = 16
PEER_CHUNK = 16


def _mm_kernel(a_ref, b_ref, o_ref):
    o_ref[...] = jnp.dot(a_ref[...].astype(jnp.bfloat16), b_ref[...],
                         preferred_element_type=jnp.float32)


def _matmul(a, b, tm=512, tn=512):
    m, k = a.shape
    _, n = b.shape
    tn = min(tn, n)
    return pl.pallas_call(
        _mm_kernel,
        grid=(pl.cdiv(m, tm), n // tn),
        in_specs=[pl.BlockSpec((tm, k), lambda i, j: (i, 0)),
                  pl.BlockSpec((k, tn), lambda i, j: (0, j))],
        out_specs=pl.BlockSpec((tm, tn), lambda i, j: (i, j)),
        out_shape=jax.ShapeDtypeStruct((m, n), jnp.float32),
    )(a, b.astype(jnp.bfloat16))


def _rmsnorm(x, g):
    y = x * lax.rsqrt(jnp.mean(x * x, axis=-1, keepdims=True) + EPS)
    return y * g


def _hyena_filter(L, w1, b1, w2, b2, w3, b3, freq):
    f32 = jnp.float32
    t = jnp.linspace(0.0, 1.0, L, dtype=f32)[:, None]
    w = (2.0 * math.pi / L) * jnp.arange(L, dtype=f32)[:, None]
    bands = jnp.linspace(1e-4, HY_BANDS - 1, HY_BANDS, dtype=f32)[None, :]
    z = jnp.concatenate([t, jnp.cos(bands * w), -jnp.sin(bands * w)], axis=-1)
    hdn = jnp.sin(freq[0] * (z @ w1 + b1))
    hdn = jnp.sin(freq[1] * (hdn @ w2 + b2))
    k = (hdn @ w3 + b3).reshape(L, 2, D_HY)
    deltas = jnp.abs(jnp.linspace(math.log(HY_TARGET) / HY_DECAY_HI,
                                  math.log(HY_TARGET) / HY_DECAY_LO, D_HY, dtype=f32))
    k = k * jnp.exp(-t * deltas)[:, None, :]
    k_fwd, k_bwd = k[:, 0], k[:, 1]
    return jnp.concatenate([k_fwd, jnp.zeros((1, D_HY), f32), k_bwd[:0:-1]], axis=0)


def _fft_long_conv(u, k_full, skip):
    L = u.shape[1]
    U = jnp.fft.rfft(u, n=2 * L, axis=1)
    K = jnp.fft.rfft(k_full, axis=0)
    y = jnp.fft.irfft(U * K[None], n=2 * L, axis=1)[:, :L]
    return y + u * skip


def _hyena_branch(hy_in, conv_w, conv_b, w1, b1, w2, b2, w3, b3, freq, skip, w_proj):
    B, L, _ = hy_in.shape
    hp = jnp.pad(hy_in, ((0, 0), (1, 1), (0, 0)))
    u = hp[:, 0:L] * conv_w[0] + hp[:, 1:L + 1] * conv_w[1] + hp[:, 2:L + 2] * conv_w[2] + conv_b
    x0, x1, v = u[..., :D_HY], u[..., D_HY:2 * D_HY], u[..., 2 * D_HY:]
    k_full = _hyena_filter(L, w1, b1, w2, b2, w3, b3, freq)
    y = _fft_long_conv(v * x1, k_full, skip) * x0
    return _matmul(y.reshape(B * L, D_HY), w_proj).reshape(B, L, D_MODEL)


def _rglru_direction(x, conv_w, conv_b, gate_w, gate_b, lam):
    B, L, _ = x.shape
    xp = jnp.pad(x, ((0, 0), (RG_CONV - 1, 0), (0, 0)))
    xc = conv_b + sum(xp[:, RG_CONV - 1 - j:RG_CONV - 1 - j + L] * conv_w[j] for j in range(RG_CONV))
    xh = xc.reshape(B, L, RG_HEADS, RG_HEAD_DIM)
    gates = jnp.einsum('blhi,ghij->gblhj', xh, gate_w).reshape(2, B, L, D_RG)
    gates = gates + gate_b[:, None, None, :]
    r = jax.nn.sigmoid(gates[0])
    i = jax.nn.sigmoid(gates[1])
    log_a = -RG_C * r * jax.nn.softplus(-lam)
    a = jnp.exp(log_a)
    mult = jnp.sqrt(-jnp.expm1(2.0 * log_a))
    mult = mult.at[:, 0].set(1.0)
    b = mult * i * xc

    def combine(p, q):
        a1, b1 = p
        a2, b2 = q
        return a1 * a2, a2 * b1 + b2

    _, h = lax.associative_scan(combine, (a, b), axis=1)
    return h


def _rglru_branch(rg_x, rg_g, conv_w, conv_b, gate_w, gate_b, lam, w_proj):
    B, L, _ = rg_x.shape
    h_f = _rglru_direction(rg_x, conv_w[0], conv_b[0], gate_w[0], gate_b[0], lam[0])
    h_b = _rglru_direction(rg_x[:, ::-1], conv_w[1], conv_b[1], gate_w[1], gate_b[1], lam[1])[:, ::-1]
    y = (h_f + h_b) * jax.nn.gelu(rg_g)
    return _matmul(y.reshape(B * L, D_RG), w_proj).reshape(B, L, D_MODEL)


def _peer(x, wq, keys, u_tab, v_tab):
    B, L, D = x.shape
    K = PEER_TOPK
    q = _matmul(x.reshape(B * L, D), wq).reshape(B, L, PEER_HEADS, 2, D_HALF)
    s = jnp.einsum('blhpd,hpkd->blhpk', q, keys)
    ts, ti = lax.top_k(s, K)
    cand = ts[..., 0, :, None] + ts[..., 1, None, :]
    cs, ci = lax.top_k(cand.reshape(B, L, PEER_HEADS, K * K), K)
    e1 = jnp.take_along_axis(ti[..., 0, :], ci // K, axis=-1)
    e2 = jnp.take_along_axis(ti[..., 1, :], ci % K, axis=-1)
    experts = e1 * N_KEYS + e2
    g = jax.nn.softmax(cs, axis=-1)

    nc = L // PEER_CHUNK
    hk = PEER_HEADS * K

    def to_chunks(a, last):
        return a.reshape(B, nc, PEER_CHUNK, last).transpose(1, 0, 2, 3).reshape(nc, B * PEER_CHUNK, last)

    xs = (to_chunks(x, D), to_chunks(experts.reshape(B, L, hk), hk), to_chunks(g.reshape(B, L, hk), hk))

    def chunk_fn(args):
        xt, et, gt = args
        act = jax.nn.gelu(jnp.einsum('td,tkd->tk', xt, u_tab[et])) * gt
        return jnp.einsum('tk,tkd->td', act, v_tab[et])

    out = lax.map(chunk_fn, xs)
    return out.reshape(nc, B, PEER_CHUNK, D).transpose(1, 0, 2, 3).reshape(B, L, D)


def kernel(x, meta, g_final, g_mix, w_in, b_gate, hy_conv_w, hy_conv_b, hy_w1, hy_b1, hy_w2, hy_b2,
           hy_w3, hy_b3, hy_freq, hy_skip, hy_out, rg_conv_w, rg_conv_b, rg_gate_w, rg_gate_b,
           rg_lambda, rg_out, w_out, g_ffn, peer_wq, peer_keys, peer_u, peer_v):
    B = x.shape[0]
    h = jnp.concatenate([jnp.broadcast_to(meta[None], (B, N_META, D_MODEL)), x], axis=1)
    L = h.shape[1]
    s_hy = 3 * D_HY
    s_rx = s_hy + D_RG
    s_rg = s_rx + D_RG
    l = 0
    n = _rmsnorm(h, g_mix[l])
    proj = _matmul(n.reshape(B * L, D_MODEL), w_in[l]).reshape(B, L, -1)
    y_hy = _hyena_branch(proj[..., :s_hy], hy_conv_w[l], hy_conv_b[l], hy_w1[l], hy_b1[l], hy_w2[l],
                         hy_b2[l], hy_w3[l], hy_b3[l], hy_freq[l], hy_skip[l], hy_out[l])
    y_rg = _rglru_branch(proj[..., s_hy:s_rx], proj[..., s_rx:s_rg], rg_conv_w[l], rg_conv_b[l],
                         rg_gate_w[l], rg_gate_b[l], rg_lambda[l], rg_out[l])
    gl = proj[..., s_rg:] + b_gate[l]
    merged = jax.nn.sigmoid(gl[..., :D_MODEL]) * y_hy + jax.nn.sigmoid(gl[..., D_MODEL:]) * y_rg
    h = h + _matmul(merged.reshape(B * L, D_MODEL), w_out[l]).reshape(B, L, D_MODEL)
    h = h + _peer(_rmsnorm(h, g_ffn[l]), peer_wq[l], peer_keys[l], peer_u[l], peer_v[l])
    h = _rmsnorm(h, g_final)
    return h[:, N_META:]
```

```python
import functools
import math

import jax
import jax.numpy as jnp
import numpy as np
from jax import lax
from jax.experimental import pallas as pl
from jax.experimental.pallas import tpu as pltpu

D_MODEL = 1024
N_META = 16
EPS = 1e-6
D_HY = 512
HY_EMB = 33
HY_BANDS = (HY_EMB - 1) // 2
HY_TARGET = 1e-2
HY_DECAY_HI = 0.3
HY_DECAY_LO = 1.5
D_RG = 1024
RG_HEADS = 8
RG_HEAD_DIM = D_RG // RG_HEADS
RG_CONV = 4
RG_C = 8.0
PEER_HEADS = 8
N_KEYS = 128
D_KEY = 256
D_HALF = D_KEY // 2
PEER_TOPK = 16
PEER_CHUNK = 16


def _mm_kernel(a_ref, b_ref, o_ref):
    o_ref[...] = jnp.dot(a_ref[...].astype(jnp.bfloat16), b_ref[...],
                         preferred_element_type=jnp.float32)


def _matmul(a, b, tm=512, tn=512):
    m, k = a.shape
    _, n = b.shape
    tn = min(tn, n)
    return pl.pallas_call(
        _mm_kernel,
        grid=(pl.cdiv(m, tm), n // tn),
        in_specs=[pl.BlockSpec((tm, k), lambda i, j: (i, 0)),
                  pl.BlockSpec((k, tn), lambda i, j: (0, j))],
        out_specs=pl.BlockSpec((tm, tn), lambda i, j: (i, j)),
        out_shape=jax.ShapeDtypeStruct((m, n), jnp.float32),
    )(a, b.astype(jnp.bfloat16))


def _rmsnorm(x, g):
    y = x * lax.rsqrt(jnp.mean(x * x, axis=-1, keepdims=True) + EPS)
    return y * g


def _hyena_filter(L, w1, b1, w2, b2, w3, b3, freq):
    f32 = jnp.float32
    t = jnp.linspace(0.0, 1.0, L, dtype=f32)[:, None]
    w = (2.0 * math.pi / L) * jnp.arange(L, dtype=f32)[:, None]
    bands = jnp.linspace(1e-4, HY_BANDS - 1, HY_BANDS, dtype=f32)[None, :]
    z = jnp.concatenate([t, jnp.cos(bands * w), -jnp.sin(bands * w)], axis=-1)
    hdn = jnp.sin(freq[0] * (z @ w1 + b1))
    hdn = jnp.sin(freq[1] * (hdn @ w2 + b2))
    k = (hdn @ w3 + b3).reshape(L, 2, D_HY)
    deltas = jnp.abs(jnp.linspace(math.log(HY_TARGET) / HY_DECAY_HI,
                                  math.log(HY_TARGET) / HY_DECAY_LO, D_HY, dtype=f32))
    k = k * jnp.exp(-t * deltas)[:, None, :]
    k_fwd, k_bwd = k[:, 0], k[:, 1]
    return jnp.concatenate([k_fwd, jnp.zeros((1, D_HY), f32), k_bwd[:0:-1]], axis=0)


def _fft_long_conv(u, k_full, skip):
    L = u.shape[1]
    U = jnp.fft.rfft(u, n=2 * L, axis=1)
    K = jnp.fft.rfft(k_full, axis=0)
    y = jnp.fft.irfft(U * K[None], n=2 * L, axis=1)[:, :L]
    return y + u * skip


def _hyena_branch(hy_in, conv_w, conv_b, w1, b1, w2, b2, w3, b3, freq, skip, w_proj):
    B, L, _ = hy_in.shape
    hp = jnp.pad(hy_in, ((0, 0), (1, 1), (0, 0)))
    u = hp[:, 0:L] * conv_w[0] + hp[:, 1:L + 1] * conv_w[1] + hp[:, 2:L + 2] * conv_w[2] + conv_b
    x0, x1, v = u[..., :D_HY], u[..., D_HY:2 * D_HY], u[..., 2 * D_HY:]
    k_full = _hyena_filter(L, w1, b1, w2, b2, w3, b3, freq)
    y = _fft_long_conv(v * x1, k_full, skip) * x0
    return _matmul(y.reshape(B * L, D_HY), w_proj).reshape(B, L, D_MODEL)


def _rglru_direction(x, conv_w, conv_b, gate_w, gate_b, lam):
    B, L, _ = x.shape
    xp = jnp.pad(x, ((0, 0), (RG_CONV - 1, 0), (0, 0)))
    xc = conv_b + sum(xp[:, RG_CONV - 1 - j:RG_CONV - 1 - j + L] * conv_w[j] for j in range(RG_CONV))
    xh = xc.reshape(B, L, RG_HEADS, RG_HEAD_DIM)
    gates = jnp.einsum('blhi,ghij->gblhj', xh, gate_w).reshape(2, B, L, D_RG)
    gates = gates + gate_b[:, None, None, :]
    r = jax.nn.sigmoid(gates[0])
    i = jax.nn.sigmoid(gates[1])
    log_a = -RG_C * r * jax.nn.softplus(-lam)
    a = jnp.exp(log_a)
    mult = jnp.sqrt(-jnp.expm1(2.0 * log_a))
    mult = mult.at[:, 0].set(1.0)
    b = mult * i * xc

    def combine(p, q):
        a1, b1 = p
        a2, b2 = q
        return a1 * a2, a2 * b1 + b2

    _, h = lax.associative_scan(combine, (a, b), axis=1)
    return h


def _rglru_branch(rg_x, rg_g, conv_w, conv_b, gate_w, gate_b, lam, w_proj):
    B, L, _ = rg_x.shape
    h_f = _rglru_direction(rg_x, conv_w[0], conv_b[0], gate_w[0], gate_b[0], lam[0])
    h_b = _rglru_direction(rg_x[:, ::-1], conv_w[1], conv_b[1], gate_w[1], gate_b[1], lam[1])[:, ::-1]
    y = (h_f + h_b) * jax.nn.gelu(rg_g)
    return _matmul(y.reshape(B * L, D_RG), w_proj).reshape(B, L, D_MODEL)


PEER_HK = PEER_HEADS * PEER_TOPK
PEER_LANES = 128
PEER_ROWS = D_MODEL // PEER_LANES
PEER_TT = 128
PEER_G = 8
PEER_NG = PEER_TT // PEER_G
PEER_NSLOT = 4
PEER_AHEAD = 2
PEER_TX = PEER_TT + PEER_AHEAD * PEER_G


def _split_bf16(x):
    hi = x.astype(jnp.bfloat16)
    lo = (x - hi.astype(jnp.float32)).astype(jnp.bfloat16)
    return hi, lo


def _peer_kernel(idx_ref, g_ref, x_ref, tab_ref, sel_ref, exp_ref, mask_ref, out_ref, *scratch):
    bufs = scratch[:PEER_NSLOT]
    sem = scratch[PEER_NSLOT]
    step = pl.program_id(0)
    last = pl.num_programs(0) - 1
    G, HK, R = PEER_G, PEER_HK, PEER_ROWS

    def issue(tok0, slot):
        for j in range(G):
            for k in range(HK):
                e = idx_ref[0, tok0 + j, k]
                pltpu.make_async_copy(tab_ref.at[e], bufs[slot].at[j * HK + k], sem.at[slot]).start()

    def wait(slot):
        pltpu.make_async_copy(tab_ref.at[pl.ds(0, G * HK)], bufs[slot], sem.at[slot]).wait()

    mask = mask_ref[...]

    def compute(tok0, slot):
        buf = bufs[slot]
        xs = x_ref[pl.ds(tok0, G)]
        vbs = []
        rms = []
        for j in range(G):
            w = buf[j * HK:(j + 1) * HK].reshape(HK * R, PEER_LANES)
            ub = pltpu.bitcast(w << 16, jnp.float32).astype(jnp.bfloat16)
            vbs.append(pltpu.bitcast(w & jnp.uint32(0xFFFF0000), jnp.float32).astype(jnp.bfloat16))
            xh, xl = _split_bf16(xs[j])
            x16 = jnp.concatenate([xh, xl], axis=0)
            rt = lax.dot_general(x16, ub, (((1,), (1,)), ((), ())), preferred_element_type=jnp.float32)
            rms.append((rt[0:R] + rt[R:]) * mask)
        rm = jnp.concatenate(rms, axis=0)
        rh, rl = _split_bf16(rm)
        z = jnp.dot(jnp.concatenate([rh, rl], axis=0), sel_ref[...], preferred_element_type=jnp.float32)
        z = z[0:G * R] + z[G * R:]
        act = jnp.sum(z.reshape(G, R, HK), axis=1)
        a = jax.nn.gelu(act) * g_ref[pl.ds(tok0, G), :]
        ah, al = _split_bf16(a)
        arep = jnp.dot(jnp.concatenate([ah, al], axis=0), exp_ref[...], preferred_element_type=jnp.float32)
        arep = arep[0:G] + arep[G:]
        for j in range(G):
            am = jnp.broadcast_to(arep[j:j + 1], (R, HK * R)) * mask
            amh, aml = _split_bf16(am)
            o = jnp.dot(jnp.concatenate([amh, aml], axis=0), vbs[j], preferred_element_type=jnp.float32)
            out_ref[tok0 + j] = o[0:R] + o[R:]

    @pl.when(step == 0)
    def _():
        for a in range(PEER_AHEAD):
            issue(a * G, a)

    def body(r, c):
        for s in range(PEER_NSLOT):
            tok0 = pl.multiple_of((r * PEER_NSLOT + s) * G, G)
            wait(s)
            issue(tok0 + PEER_AHEAD * G, (s + PEER_AHEAD) % PEER_NSLOT)
            compute(tok0, s)
        return c

    lax.fori_loop(0, PEER_NG // PEER_NSLOT, body, 0)

    @pl.when(step == last)
    def _():
        for a in range(PEER_AHEAD):
            wait(a)


def _peer_experts(idx, g, x3, table):
    T = idx.shape[0]
    ns = T // PEER_TT
    HK, R = PEER_HK, PEER_ROWS
    r = np.arange(HK * R)
    sel = jnp.asarray(r[:, None] // R == np.arange(HK)[None, :], jnp.bfloat16)
    mask = jnp.asarray(r[None, :] % R == np.arange(R)[:, None], jnp.float32)
    idx3 = idx.reshape(ns, PEER_TT, HK)
    ahead = PEER_AHEAD * PEER_G
    nxt = jnp.concatenate([idx3[1:, :ahead], idx3[-1:, :ahead]], axis=0)
    idx_ext = jnp.concatenate([idx3, nxt], axis=1)
    return pl.pallas_call(
        _peer_kernel,
        grid=(ns,),
        in_specs=[
            pl.BlockSpec((1, PEER_TX, HK), lambda i: (i, 0, 0), memory_space=pltpu.SMEM),
            pl.BlockSpec((PEER_TT, HK), lambda i: (i, 0)),
            pl.BlockSpec((PEER_TT, R, PEER_LANES), lambda i: (i, 0, 0)),
            pl.BlockSpec(memory_space=pl.ANY),
            pl.BlockSpec((HK * R, HK), lambda i: (0, 0)),
            pl.BlockSpec((HK, HK * R), lambda i: (0, 0)),
            pl.BlockSpec((R, HK * R), lambda i: (0, 0)),
        ],
        out_specs=pl.BlockSpec((PEER_TT, R, PEER_LANES), lambda i: (i, 0, 0)),
        out_shape=jax.ShapeDtypeStruct((T, R, PEER_LANES), jnp.float32),
        scratch_shapes=[pltpu.VMEM((PEER_G * HK, R, PEER_LANES), jnp.uint32) for _ in range(PEER_NSLOT)]
        + [pltpu.SemaphoreType.DMA((PEER_NSLOT,))],
        compiler_params=pltpu.CompilerParams(dimension_semantics=("arbitrary",)),
        name="peer_experts",
    )(idx_ext, g, x3, table, sel, sel.T, mask)


def _pack_expert_tables(u, v):
    ub = lax.bitcast_convert_type(u.astype(jnp.bfloat16), jnp.uint16).astype(jnp.uint32)
    vb = lax.bitcast_convert_type(v.astype(jnp.bfloat16), jnp.uint16).astype(jnp.uint32)
    return ((vb << 16) | ub).reshape(u.shape[0], PEER_ROWS, PEER_LANES)


def _peer(x, wq, keys, u_tab, v_tab):
    B, L, D = x.shape
    K = PEER_TOPK
    q = _matmul(x.reshape(B * L, D), wq).reshape(B, L, PEER_HEADS, 2, D_HALF)
    s = jnp.einsum('blhpd,hpkd->blhpk', q, keys)
    ts, ti = lax.top_k(s, K)
    cand = ts[..., 0, :, None] + ts[..., 1, None, :]
    cs, ci = lax.top_k(cand.reshape(B, L, PEER_HEADS, K * K), K)
    e1 = jnp.take_along_axis(ti[..., 0, :], ci // K, axis=-1)
    e2 = jnp.take_along_axis(ti[..., 1, :], ci % K, axis=-1)
    experts = e1 * N_KEYS + e2
    g = jax.nn.softmax(cs, axis=-1)
    hk = PEER_HEADS * K
    out = _peer_experts(experts.reshape(B * L, hk).astype(jnp.int32), g.reshape(B * L, hk),
                        x.reshape(B * L, PEER_ROWS, PEER_LANES), _pack_expert_tables(u_tab, v_tab))
    return out.reshape(B, L, D)


def kernel(x, meta, g_final, g_mix, w_in, b_gate, hy_conv_w, hy_conv_b, hy_w1, hy_b1, hy_w2, hy_b2,
           hy_w3, hy_b3, hy_freq, hy_skip, hy_out, rg_conv_w, rg_conv_b, rg_gate_w, rg_gate_b,
           rg_lambda, rg_out, w_out, g_ffn, peer_wq, peer_keys, peer_u, peer_v):
    B = x.shape[0]
    h = jnp.concatenate([jnp.broadcast_to(meta[None], (B, N_META, D_MODEL)), x], axis=1)
    L = h.shape[1]
    s_hy = 3 * D_HY
    s_rx = s_hy + D_RG
    s_rg = s_rx + D_RG
    l = 0
    n = _rmsnorm(h, g_mix[l])
    proj = _matmul(n.reshape(B * L, D_MODEL), w_in[l]).reshape(B, L, -1)
    y_hy = _hyena_branch(proj[..., :s_hy], hy_conv_w[l], hy_conv_b[l], hy_w1[l], hy_b1[l], hy_w2[l],
                         hy_b2[l], hy_w3[l], hy_b3[l], hy_freq[l], hy_skip[l], hy_out[l])
    y_rg = _rglru_branch(proj[..., s_hy:s_rx], proj[..., s_rx:s_rg], rg_conv_w[l], rg_conv_b[l],
                         rg_gate_w[l], rg_gate_b[l], rg_lambda[l], rg_out[l])
    gl = proj[..., s_rg:] + b_gate[l]
    merged = jax.nn.sigmoid(gl[..., :D_MODEL]) * y_hy + jax.nn.sigmoid(gl[..., D_MODEL:]) * y_rg
    h = h + _matmul(merged.reshape(B * L, D_MODEL), w_out[l]).reshape(B, L, D_MODEL)
    h = h + _peer(_rmsnorm(h, g_ffn[l]), peer_wq[l], peer_keys[l], peer_u[l], peer_v[l])
    h = _rmsnorm(h, g_final)
    return h[:, N_META:]
```

```python
import functools
import math

import jax
import jax.numpy as jnp
import numpy as np
from jax import lax
from jax.experimental import pallas as pl
from jax.experimental.pallas import tpu as pltpu

D_MODEL = 1024
N_META = 16
EPS = 1e-6
D_HY = 512
HY_EMB = 33
HY_BANDS = (HY_EMB - 1) // 2
HY_TARGET = 1e-2
HY_DECAY_HI = 0.3
HY_DECAY_LO = 1.5
D_RG = 1024
RG_HEADS = 8
RG_HEAD_DIM = D_RG // RG_HEADS
RG_CONV = 4
RG_C = 8.0
PEER_HEADS = 8
N_KEYS = 128
D_KEY = 256
D_HALF = D_KEY // 2
PEER_TOPK = 16
PEER_CHUNK = 16


def _mm_kernel(a_ref, b_ref, o_ref):
    o_ref[...] = jnp.dot(a_ref[...].astype(jnp.bfloat16), b_ref[...],
                         preferred_element_type=jnp.float32)


def _matmul(a, b, tm=512, tn=512):
    m, k = a.shape
    _, n = b.shape
    tn = min(tn, n)
    return pl.pallas_call(
        _mm_kernel,
        grid=(pl.cdiv(m, tm), n // tn),
        in_specs=[pl.BlockSpec((tm, k), lambda i, j: (i, 0)),
                  pl.BlockSpec((k, tn), lambda i, j: (0, j))],
        out_specs=pl.BlockSpec((tm, tn), lambda i, j: (i, j)),
        out_shape=jax.ShapeDtypeStruct((m, n), jnp.float32),
    )(a, b.astype(jnp.bfloat16))


def _rmsnorm(x, g):
    y = x * lax.rsqrt(jnp.mean(x * x, axis=-1, keepdims=True) + EPS)
    return y * g


def _hyena_filter(L, w1, b1, w2, b2, w3, b3, freq):
    f32 = jnp.float32
    t = jnp.linspace(0.0, 1.0, L, dtype=f32)[:, None]
    w = (2.0 * math.pi / L) * jnp.arange(L, dtype=f32)[:, None]
    bands = jnp.linspace(1e-4, HY_BANDS - 1, HY_BANDS, dtype=f32)[None, :]
    z = jnp.concatenate([t, jnp.cos(bands * w), -jnp.sin(bands * w)], axis=-1)
    hdn = jnp.sin(freq[0] * (z @ w1 + b1))
    hdn = jnp.sin(freq[1] * (hdn @ w2 + b2))
    k = (hdn @ w3 + b3).reshape(L, 2, D_HY)
    deltas = jnp.abs(jnp.linspace(math.log(HY_TARGET) / HY_DECAY_HI,
                                  math.log(HY_TARGET) / HY_DECAY_LO, D_HY, dtype=f32))
    k = k * jnp.exp(-t * deltas)[:, None, :]
    k_fwd, k_bwd = k[:, 0], k[:, 1]
    return jnp.concatenate([k_fwd, jnp.zeros((1, D_HY), f32), k_bwd[:0:-1]], axis=0)


def _fft_long_conv(u, k_full, skip):
    L = u.shape[1]
    U = jnp.fft.rfft(u, n=2 * L, axis=1)
    K = jnp.fft.rfft(k_full, axis=0)
    y = jnp.fft.irfft(U * K[None], n=2 * L, axis=1)[:, :L]
    return y + u * skip


def _hyena_branch(hy_in, conv_w, conv_b, w1, b1, w2, b2, w3, b3, freq, skip, w_proj):
    B, L, _ = hy_in.shape
    hp = jnp.pad(hy_in, ((0, 0), (1, 1), (0, 0)))
    u = hp[:, 0:L] * conv_w[0] + hp[:, 1:L + 1] * conv_w[1] + hp[:, 2:L + 2] * conv_w[2] + conv_b
    x0, x1, v = u[..., :D_HY], u[..., D_HY:2 * D_HY], u[..., 2 * D_HY:]
    k_full = _hyena_filter(L, w1, b1, w2, b2, w3, b3, freq)
    y = _fft_long_conv(v * x1, k_full, skip) * x0
    return _matmul(y.reshape(B * L, D_HY), w_proj).reshape(B, L, D_MODEL)


def _rglru_direction(x, conv_w, conv_b, gate_w, gate_b, lam):
    B, L, _ = x.shape
    xp = jnp.pad(x, ((0, 0), (RG_CONV - 1, 0), (0, 0)))
    xc = conv_b + sum(xp[:, RG_CONV - 1 - j:RG_CONV - 1 - j + L] * conv_w[j] for j in range(RG_CONV))
    xh = xc.reshape(B, L, RG_HEADS, RG_HEAD_DIM)
    gates = jnp.einsum('blhi,ghij->gblhj', xh, gate_w).reshape(2, B, L, D_RG)
    gates = gates + gate_b[:, None, None, :]
    r = jax.nn.sigmoid(gates[0])
    i = jax.nn.sigmoid(gates[1])
    log_a = -RG_C * r * jax.nn.softplus(-lam)
    a = jnp.exp(log_a)
    mult = jnp.sqrt(-jnp.expm1(2.0 * log_a))
    mult = mult.at[:, 0].set(1.0)
    b = mult * i * xc

    def combine(p, q):
        a1, b1 = p
        a2, b2 = q
        return a1 * a2, a2 * b1 + b2

    _, h = lax.associative_scan(combine, (a, b), axis=1)
    return h


def _rglru_branch(rg_x, rg_g, conv_w, conv_b, gate_w, gate_b, lam, w_proj):
    B, L, _ = rg_x.shape
    h_f = _rglru_direction(rg_x, conv_w[0], conv_b[0], gate_w[0], gate_b[0], lam[0])
    h_b = _rglru_direction(rg_x[:, ::-1], conv_w[1], conv_b[1], gate_w[1], gate_b[1], lam[1])[:, ::-1]
    y = (h_f + h_b) * jax.nn.gelu(rg_g)
    return _matmul(y.reshape(B * L, D_RG), w_proj).reshape(B, L, D_MODEL)


PEER_HK = PEER_HEADS * PEER_TOPK
PEER_LANES = 128
PEER_ROWS = D_MODEL // PEER_LANES
PEER_TT = 128
PEER_G = 8
PEER_NG = PEER_TT // PEER_G
PEER_NSLOT = 4
PEER_AHEAD = 2
PEER_TX = PEER_TT + PEER_AHEAD * PEER_G


def _split_bf16(x):
    hi = x.astype(jnp.bfloat16)
    lo = (x - hi.astype(jnp.float32)).astype(jnp.bfloat16)
    return hi, lo


def _peer_kernel(idx_ref, g_ref, x_ref, tab_ref, sel_ref, exp_ref, mask_ref, out_ref, *scratch):
    bufs = scratch[:PEER_NSLOT]
    sem = scratch[PEER_NSLOT]
    step = pl.program_id(0)
    last = pl.num_programs(0) - 1
    G, HK, R = PEER_G, PEER_HK, PEER_ROWS

    def issue(tok0, slot):
        for j in range(G):
            for k in range(HK):
                e = idx_ref[0, tok0 + j, k]
                pltpu.make_async_copy(tab_ref.at[e], bufs[slot].at[j * HK + k], sem.at[slot]).start(priority=k % 2)

    def wait(slot):
        pltpu.make_async_copy(tab_ref.at[pl.ds(0, G * HK)], bufs[slot], sem.at[slot]).wait()

    mask = mask_ref[...]

    def compute(tok0, slot):
        buf = bufs[slot]
        xs = x_ref[pl.ds(tok0, G)]
        vbs = []
        rms = []
        for j in range(G):
            w = buf[j * HK:(j + 1) * HK].reshape(HK * R, PEER_LANES)
            ub = pltpu.bitcast(w << 16, jnp.float32).astype(jnp.bfloat16)
            vbs.append(pltpu.bitcast(w & jnp.uint32(0xFFFF0000), jnp.float32).astype(jnp.bfloat16))
            xh, xl = _split_bf16(xs[j])
            x16 = jnp.concatenate([xh, xl], axis=0)
            rt = lax.dot_general(x16, ub, (((1,), (1,)), ((), ())), preferred_element_type=jnp.float32)
            rms.append((rt[0:R] + rt[R:]) * mask)
        rm = jnp.concatenate(rms, axis=0)
        rh, rl = _split_bf16(rm)
        z = jnp.dot(jnp.concatenate([rh, rl], axis=0), sel_ref[...], preferred_element_type=jnp.float32)
        z = z[0:G * R] + z[G * R:]
        act = jnp.sum(z.reshape(G, R, HK), axis=1)
        a = jax.nn.gelu(act) * g_ref[pl.ds(tok0, G), :]
        ah, al = _split_bf16(a)
        arep = jnp.dot(jnp.concatenate([ah, al], axis=0), exp_ref[...], preferred_element_type=jnp.float32)
        arep = arep[0:G] + arep[G:]
        for j in range(G):
            am = jnp.broadcast_to(arep[j:j + 1], (R, HK * R)) * mask
            amh, aml = _split_bf16(am)
            o = jnp.dot(jnp.concatenate([amh, aml], axis=0), vbs[j], preferred_element_type=jnp.float32)
            out_ref[tok0 + j] = o[0:R] + o[R:]

    @pl.when(step == 0)
    def _():
        for a in range(PEER_AHEAD):
            issue(a * G, a)

    def body(r, c):
        for s in range(PEER_NSLOT):
            tok0 = pl.multiple_of((r * PEER_NSLOT + s) * G, G)
            wait(s)
            issue(tok0 + PEER_AHEAD * G, (s + PEER_AHEAD) % PEER_NSLOT)
            compute(tok0, s)
        return c

    lax.fori_loop(0, PEER_NG // PEER_NSLOT, body, 0)

    @pl.when(step == last)
    def _():
        for a in range(PEER_AHEAD):
            wait(a)


def _peer_experts(idx, g, x3, table):
    T = idx.shape[0]
    ns = T // PEER_TT
    HK, R = PEER_HK, PEER_ROWS
    r = np.arange(HK * R)
    sel = jnp.asarray(r[:, None] // R == np.arange(HK)[None, :], jnp.bfloat16)
    mask = jnp.asarray(r[None, :] % R == np.arange(R)[:, None], jnp.float32)
    idx3 = idx.reshape(ns, PEER_TT, HK)
    ahead = PEER_AHEAD * PEER_G
    nxt = jnp.concatenate([idx3[1:, :ahead], idx3[-1:, :ahead]], axis=0)
    idx_ext = jnp.concatenate([idx3, nxt], axis=1)
    return pl.pallas_call(
        _peer_kernel,
        grid=(ns,),
        in_specs=[
            pl.BlockSpec((1, PEER_TX, HK), lambda i: (i, 0, 0), memory_space=pltpu.SMEM),
            pl.BlockSpec((PEER_TT, HK), lambda i: (i, 0)),
            pl.BlockSpec((PEER_TT, R, PEER_LANES), lambda i: (i, 0, 0)),
            pl.BlockSpec(memory_space=pl.ANY),
            pl.BlockSpec((HK * R, HK), lambda i: (0, 0)),
            pl.BlockSpec((HK, HK * R), lambda i: (0, 0)),
            pl.BlockSpec((R, HK * R), lambda i: (0, 0)),
        ],
        out_specs=pl.BlockSpec((PEER_TT, R, PEER_LANES), lambda i: (i, 0, 0)),
        out_shape=jax.ShapeDtypeStruct((T, R, PEER_LANES), jnp.float32),
        scratch_shapes=[pltpu.VMEM((PEER_G * HK, R, PEER_LANES), jnp.uint32) for _ in range(PEER_NSLOT)]
        + [pltpu.SemaphoreType.DMA((PEER_NSLOT,))],
        compiler_params=pltpu.CompilerParams(dimension_semantics=("arbitrary",)),
        name="peer_experts",
    )(idx_ext, g, x3, table, sel, sel.T, mask)


def _pack_expert_tables(u, v):
    ub = lax.bitcast_convert_type(u.astype(jnp.bfloat16), jnp.uint16).astype(jnp.uint32)
    vb = lax.bitcast_convert_type(v.astype(jnp.bfloat16), jnp.uint16).astype(jnp.uint32)
    return ((vb << 16) | ub).reshape(u.shape[0], PEER_ROWS, PEER_LANES)


ROUTE_TT = 128
ROUTE_GROUP = 8


def _cand_layout():
    K = PEER_TOPK
    ii, jj, ok = [], [], []

    def add(i_list, j_list, ok_list):
        ii.extend(i_list)
        jj.extend(j_list)
        ok.extend(ok_list)

    add([0] * 8, list(range(8)), [True] * 8)
    add([0] * 8, list(range(8, 16)), [True] * 8)
    for i in range(1, 8):
        jmax = K // (i + 1) - 1
        add([i] * 8, list(range(8)), [j <= jmax for j in range(8)])
    add(list(range(8, 16)), [0] * 8, [True] * 8)
    return np.array(ii), np.array(jj), np.array(ok)


_CAND_I, _CAND_J, _CAND_OK = _cand_layout()
ROUTE_NC = len(_CAND_I)


def _top16(s):
    K, TT = PEER_TOPK, ROUTE_TT
    kidx = lax.broadcasted_iota(jnp.int32, (N_KEYS, TT), 0)
    sub16 = lax.broadcasted_iota(jnp.int32, (K, TT), 0)
    ninf = jnp.float32(-jnp.inf)
    ts = jnp.zeros((K, TT), jnp.float32)
    ti = jnp.zeros((K, TT), jnp.int32)
    for r in range(K):
        m = jnp.max(s, axis=0, keepdims=True)
        imin = jnp.min(jnp.where(s == m, kidx, N_KEYS), axis=0, keepdims=True)
        s = jnp.where(kidx == imin, ninf, s)
        ts = jnp.where(sub16 == r, m, ts)
        ti = jnp.where(sub16 == r, imin, ti)
    return ts, ti


def _head_route(s0, s1, flat, neg):
    K, TT = PEER_TOPK, ROUTE_TT
    sub16 = lax.broadcasted_iota(jnp.int32, (K, TT), 0)
    ninf = jnp.float32(-jnp.inf)
    ts0, ti0 = _top16(s0)
    ts1, ti1 = _top16(s1)
    e0 = ti0 * N_KEYS

    def row(a, i):
        return jnp.broadcast_to(a[i:i + 1, :], (ROUTE_GROUP, TT))

    cv = [row(ts0, 0) + ts1[0:8], row(ts0, 0) + ts1[8:16]]
    ev = [row(e0, 0) + ti1[0:8], row(e0, 0) + ti1[8:16]]
    for i in range(1, 8):
        cv.append(row(ts0, i) + ts1[0:8])
        ev.append(row(e0, i) + ti1[0:8])
    cv.append(ts0[8:16] + row(ts1, 0))
    ev.append(e0[8:16] + row(ti1, 0))
    cand = jnp.concatenate(cv, axis=0) + neg
    eid = jnp.concatenate(ev, axis=0)
    cs = jnp.zeros((K, TT), jnp.float32)
    ei = jnp.zeros((K, TT), jnp.int32)
    m0 = None
    for r in range(K):
        m = jnp.max(cand, axis=0, keepdims=True)
        if r == 0:
            m0 = m
        fmin = jnp.min(jnp.where(cand == m, flat, 4 * K * K), axis=0, keepdims=True)
        hit = flat == fmin
        e = jnp.max(jnp.where(hit, eid, -1), axis=0, keepdims=True)
        cand = jnp.where(hit, ninf, cand)
        cs = jnp.where(sub16 == r, m, cs)
        ei = jnp.where(sub16 == r, e, ei)
    p = jnp.exp(cs - m0)
    g = p / jnp.sum(p, axis=0, keepdims=True)
    return ei, g


def _route_kernel(h_ref, gn_ref, ws_ref, flat_ref, neg_ref, xn_ref, idx_ref, g_ref, s_ref, it_ref, gt_ref):
    x = h_ref[...]
    xn = x * lax.rsqrt(jnp.mean(x * x, axis=-1, keepdims=True) + EPS) * gn_ref[...]
    xn_ref[...] = xn
    s_ref[...] = lax.dot_general(ws_ref[...], xn.astype(jnp.bfloat16), (((1,), (1,)), ((), ())),
                                 preferred_element_type=jnp.float32)
    flat = flat_ref[...]
    neg = neg_ref[...]

    def head(h, c):
        base = pl.multiple_of(h * 2 * N_KEYS, 2 * N_KEYS)
        ei, g = _head_route(s_ref[pl.ds(base, N_KEYS), :], s_ref[pl.ds(base + N_KEYS, N_KEYS), :], flat, neg)
        ob = pl.multiple_of(h * PEER_TOPK, PEER_TOPK)
        it_ref[pl.ds(ob, PEER_TOPK), :] = ei
        gt_ref[pl.ds(ob, PEER_TOPK), :] = g
        return c

    lax.fori_loop(0, PEER_HEADS, head, 0)
    idx_ref[...] = it_ref[...].T
    g_ref[...] = gt_ref[...].T


def _peer_route(h2, g_ffn, ws_t):
    T = h2.shape[0]
    TT, K, NC, HK = ROUTE_TT, PEER_TOPK, ROUTE_NC, PEER_HK
    nrow = PEER_HEADS * 2 * N_KEYS
    flat = jnp.asarray(np.broadcast_to((_CAND_I * K + _CAND_J)[:, None], (NC, TT)), jnp.int32)
    neg = jnp.asarray(np.broadcast_to(np.where(_CAND_OK, 0.0, -np.inf)[:, None], (NC, TT)), jnp.float32)
    return pl.pallas_call(
        _route_kernel,
        grid=(T // TT,),
        in_specs=[
            pl.BlockSpec((TT, D_MODEL), lambda i: (i, 0)),
            pl.BlockSpec((1, D_MODEL), lambda i: (0, 0)),
            pl.BlockSpec((nrow, D_MODEL), lambda i: (0, 0)),
            pl.BlockSpec((NC, TT), lambda i: (0, 0)),
            pl.BlockSpec((NC, TT), lambda i: (0, 0)),
        ],
        out_specs=[
            pl.BlockSpec((TT, D_MODEL), lambda i: (i, 0)),
            pl.BlockSpec((TT, HK), lambda i: (i, 0)),
            pl.BlockSpec((TT, HK), lambda i: (i, 0)),
        ],
        out_shape=[
            jax.ShapeDtypeStruct((T, D_MODEL), jnp.float32),
            jax.ShapeDtypeStruct((T, HK), jnp.int32),
            jax.ShapeDtypeStruct((T, HK), jnp.float32),
        ],
        scratch_shapes=[pltpu.VMEM((nrow, TT), jnp.float32),
                        pltpu.VMEM((HK, TT), jnp.int32),
                        pltpu.VMEM((HK, TT), jnp.float32)],
        compiler_params=pltpu.CompilerParams(dimension_semantics=("arbitrary",)),
        name="peer_route",
    )(h2, g_ffn.reshape(1, D_MODEL), ws_t, flat, neg)


def _score_weights(wq, keys):
    n = PEER_HEADS * 2
    eye = jnp.eye(n, dtype=keys.dtype)
    kd = keys.reshape(n, N_KEYS, D_HALF)
    block_diag = (eye[:, None, :, None] * kd[:, :, None, :]).reshape(n * N_KEYS, n * D_HALF)
    return _matmul(block_diag, wq.T, tm=512, tn=512).astype(jnp.bfloat16)


def _peer(h2, g_ffn, wq, keys, u_tab, v_tab):
    T = h2.shape[0]
    xn, experts, g = _peer_route(h2, g_ffn, _score_weights(wq, keys))
    out = _peer_experts(experts, g, xn.reshape(T, PEER_ROWS, PEER_LANES), _pack_expert_tables(u_tab, v_tab))
    return out.reshape(T, D_MODEL)


def kernel(x, meta, g_final, g_mix, w_in, b_gate, hy_conv_w, hy_conv_b, hy_w1, hy_b1, hy_w2, hy_b2,
           hy_w3, hy_b3, hy_freq, hy_skip, hy_out, rg_conv_w, rg_conv_b, rg_gate_w, rg_gate_b,
           rg_lambda, rg_out, w_out, g_ffn, peer_wq, peer_keys, peer_u, peer_v):
    B = x.shape[0]
    h = jnp.concatenate([jnp.broadcast_to(meta[None], (B, N_META, D_MODEL)), x], axis=1)
    L = h.shape[1]
    s_hy = 3 * D_HY
    s_rx = s_hy + D_RG
    s_rg = s_rx + D_RG
    l = 0
    n = _rmsnorm(h, g_mix[l])
    proj = _matmul(n.reshape(B * L, D_MODEL), w_in[l]).reshape(B, L, -1)
    y_hy = _hyena_branch(proj[..., :s_hy], hy_conv_w[l], hy_conv_b[l], hy_w1[l], hy_b1[l], hy_w2[l],
                         hy_b2[l], hy_w3[l], hy_b3[l], hy_freq[l], hy_skip[l], hy_out[l])
    y_rg = _rglru_branch(proj[..., s_hy:s_rx], proj[..., s_rx:s_rg], rg_conv_w[l], rg_conv_b[l],
                         rg_gate_w[l], rg_gate_b[l], rg_lambda[l], rg_out[l])
    gl = proj[..., s_rg:] + b_gate[l]
    merged = jax.nn.sigmoid(gl[..., :D_MODEL]) * y_hy + jax.nn.sigmoid(gl[..., D_MODEL:]) * y_rg
    h = h + _matmul(merged.reshape(B * L, D_MODEL), w_out[l]).reshape(B, L, D_MODEL)
    h2 = h.reshape(B * L, D_MODEL)
    h = (h2 + _peer(h2, g_ffn[l], peer_wq[l], peer_keys[l], peer_u[l], peer_v[l])).reshape(B, L, D_MODEL)
    h = _rmsnorm(h, g_final)
    return h[:, N_META:]
```

```python
import functools
import math

import jax
import jax.numpy as jnp
import numpy as np
from jax import lax
from jax.experimental import pallas as pl
from jax.experimental.pallas import tpu as pltpu

D_MODEL = 1024
N_META = 16
EPS = 1e-6
D_HY = 512
HY_EMB = 33
HY_BANDS = (HY_EMB - 1) // 2
HY_TARGET = 1e-2
HY_DECAY_HI = 0.3
HY_DECAY_LO = 1.5
D_RG = 1024
RG_HEADS = 8
RG_HEAD_DIM = D_RG // RG_HEADS
RG_CONV = 4
RG_C = 8.0
PEER_HEADS = 8
N_KEYS = 128
D_KEY = 256
D_HALF = D_KEY // 2
PEER_TOPK = 16
PEER_CHUNK = 16


def _mm_kernel(a_ref, b_ref, o_ref):
    o_ref[...] = jnp.dot(a_ref[...].astype(jnp.bfloat16), b_ref[...],
                         preferred_element_type=jnp.float32)


def _matmul(a, b, tm=512, tn=512):
    m, k = a.shape
    _, n = b.shape
    tn = min(tn, n)
    return pl.pallas_call(
        _mm_kernel,
        grid=(pl.cdiv(m, tm), n // tn),
        in_specs=[pl.BlockSpec((tm, k), lambda i, j: (i, 0)),
                  pl.BlockSpec((k, tn), lambda i, j: (0, j))],
        out_specs=pl.BlockSpec((tm, tn), lambda i, j: (i, j)),
        out_shape=jax.ShapeDtypeStruct((m, n), jnp.float32),
    )(a, b.astype(jnp.bfloat16))


def _rmsnorm(x, g):
    y = x * lax.rsqrt(jnp.mean(x * x, axis=-1, keepdims=True) + EPS)
    return y * g


def _hyena_filter(L, w1, b1, w2, b2, w3, b3, freq):
    f32 = jnp.float32
    t = jnp.linspace(0.0, 1.0, L, dtype=f32)[:, None]
    w = (2.0 * math.pi / L) * jnp.arange(L, dtype=f32)[:, None]
    bands = jnp.linspace(1e-4, HY_BANDS - 1, HY_BANDS, dtype=f32)[None, :]
    z = jnp.concatenate([t, jnp.cos(bands * w), -jnp.sin(bands * w)], axis=-1)
    hdn = jnp.sin(freq[0] * (z @ w1 + b1))
    hdn = jnp.sin(freq[1] * (hdn @ w2 + b2))
    k = (hdn @ w3 + b3).reshape(L, 2, D_HY)
    deltas = jnp.abs(jnp.linspace(math.log(HY_TARGET) / HY_DECAY_HI,
                                  math.log(HY_TARGET) / HY_DECAY_LO, D_HY, dtype=f32))
    k = k * jnp.exp(-t * deltas)[:, None, :]
    k_fwd, k_bwd = k[:, 0], k[:, 1]
    return jnp.concatenate([k_fwd, jnp.zeros((1, D_HY), f32), k_bwd[:0:-1]], axis=0)


def _fft_long_conv(u, k_full, skip):
    L = u.shape[1]
    U = jnp.fft.rfft(u, n=2 * L, axis=1)
    K = jnp.fft.rfft(k_full, axis=0)
    y = jnp.fft.irfft(U * K[None], n=2 * L, axis=1)[:, :L]
    return y + u * skip


def _hyena_branch(hy_in, conv_w, conv_b, w1, b1, w2, b2, w3, b3, freq, skip, w_proj):
    B, L, _ = hy_in.shape
    hp = jnp.pad(hy_in, ((0, 0), (1, 1), (0, 0)))
    u = hp[:, 0:L] * conv_w[0] + hp[:, 1:L + 1] * conv_w[1] + hp[:, 2:L + 2] * conv_w[2] + conv_b
    x0, x1, v = u[..., :D_HY], u[..., D_HY:2 * D_HY], u[..., 2 * D_HY:]
    k_full = _hyena_filter(L, w1, b1, w2, b2, w3, b3, freq)
    y = _fft_long_conv(v * x1, k_full, skip) * x0
    return _matmul(y.reshape(B * L, D_HY), w_proj).reshape(B, L, D_MODEL)


RG_TL = 257


def _rglru_kernel(x_ref, cw_ref, cb_ref, gw_ref, gb_ref, lam_ref, o_ref, halo_ref, h_ref, a_ref, b_ref,
                  *, reverse, nb):
    c = pl.program_id(1)
    rows = x_ref.shape[0]
    tl = rows // nb
    hr = (RG_CONV - 1) * nb

    @pl.when(c == 0)
    def _():
        halo_ref[...] = jnp.zeros_like(halo_ref)
        h_ref[...] = jnp.zeros_like(h_ref)

    x = x_ref[...]
    if reverse:
        xe = jnp.concatenate([x, halo_ref[...]], axis=0)
        xc = cb_ref[...] + sum(xe[j * nb:j * nb + rows] * cw_ref[j:j + 1, :] for j in range(RG_CONV))
        halo_ref[...] = x[0:hr]
    else:
        xe = jnp.concatenate([halo_ref[...], x], axis=0)
        xc = cb_ref[...] + sum(xe[hr - j * nb:hr - j * nb + rows] * cw_ref[j:j + 1, :] for j in range(RG_CONV))
        halo_ref[...] = x[rows - hr:rows]
    xb = xc.astype(jnp.bfloat16)
    gr = jnp.dot(xb, gw_ref[0, 0], preferred_element_type=jnp.float32) + gb_ref[0:1, :]
    gi = jnp.dot(xb, gw_ref[1, 0], preferred_element_type=jnp.float32) + gb_ref[1:2, :]
    r = jax.nn.sigmoid(gr)
    i = jax.nn.sigmoid(gi)
    log_a = -RG_C * r * jax.nn.softplus(-lam_ref[...])
    a = jnp.exp(log_a)
    mult = jnp.sqrt(1.0 - jnp.exp(2.0 * log_a))
    row = lax.broadcasted_iota(jnp.int32, (rows, RG_HEAD_DIM), 0)
    first_rows = (row >= rows - nb) if reverse else (row < nb)
    mult = jnp.where(jnp.logical_and(first_rows, c == 0), 1.0, mult)
    a_ref[...] = a
    b_ref[...] = mult * i * xc

    def step(t, h):
        tt = (tl - 1 - t) if reverse else t
        r0 = pl.multiple_of(tt * nb, nb)
        h = a_ref[pl.ds(r0, nb), :] * h + b_ref[pl.ds(r0, nb), :]
        o_ref[pl.ds(r0, nb), :] = h
        return h

    h_ref[...] = lax.fori_loop(0, tl, step, h_ref[...], unroll=8)


def _rglru_direction(xt, nb, conv_w, conv_b, gate_w, gate_b, lam, reverse):
    T = xt.shape[0]
    L = T // nb
    assert L % RG_TL == 0 and nb % 8 == 0
    nch = L // RG_TL
    rows = RG_TL * nb
    cmap = (lambda c: nch - 1 - c) if reverse else (lambda c: c)
    return pl.pallas_call(
        functools.partial(_rglru_kernel, reverse=reverse, nb=nb),
        grid=(RG_HEADS, nch),
        in_specs=[
            pl.BlockSpec((rows, RG_HEAD_DIM), lambda h, c: (cmap(c), h)),
            pl.BlockSpec((RG_CONV, RG_HEAD_DIM), lambda h, c: (0, h)),
            pl.BlockSpec((1, RG_HEAD_DIM), lambda h, c: (0, h)),
            pl.BlockSpec((2, 1, RG_HEAD_DIM, RG_HEAD_DIM), lambda h, c: (0, h, 0, 0)),
            pl.BlockSpec((2, RG_HEAD_DIM), lambda h, c: (0, h)),
            pl.BlockSpec((1, RG_HEAD_DIM), lambda h, c: (0, h)),
        ],
        out_specs=pl.BlockSpec((rows, RG_HEAD_DIM), lambda h, c: (cmap(c), h)),
        out_shape=jax.ShapeDtypeStruct((T, D_RG), jnp.float32),
        scratch_shapes=[pltpu.VMEM(((RG_CONV - 1) * nb, RG_HEAD_DIM), jnp.float32),
                        pltpu.VMEM((nb, RG_HEAD_DIM), jnp.float32),
                        pltpu.VMEM((rows, RG_HEAD_DIM), jnp.float32),
                        pltpu.VMEM((rows, RG_HEAD_DIM), jnp.float32)],
        compiler_params=pltpu.CompilerParams(dimension_semantics=("arbitrary", "arbitrary")),
        name="rglru_bwd" if reverse else "rglru_fwd",
    )(xt, conv_w, conv_b.reshape(1, D_RG), gate_w.astype(jnp.bfloat16), gate_b, lam.reshape(1, D_RG))


def _rglru_branch(rg_x, rg_g, conv_w, conv_b, gate_w, gate_b, lam, w_proj):
    B, L, _ = rg_x.shape
    xt = rg_x.transpose(1, 0, 2).reshape(L * B, D_RG)
    h_f = _rglru_direction(xt, B, conv_w[0], conv_b[0], gate_w[0], gate_b[0], lam[0], False)
    h_b = _rglru_direction(xt, B, conv_w[1], conv_b[1], gate_w[1], gate_b[1], lam[1], True)
    h = (h_f + h_b).reshape(L, B, D_RG).transpose(1, 0, 2)
    y = h * jax.nn.gelu(rg_g)
    return _matmul(y.reshape(B * L, D_RG), w_proj).reshape(B, L, D_MODEL)


PEER_HK = PEER_HEADS * PEER_TOPK
PEER_LANES = 128
PEER_ROWS = D_MODEL // PEER_LANES
PEER_TT = 128
PEER_G = 8
PEER_NG = PEER_TT // PEER_G
PEER_NSLOT = 4
PEER_AHEAD = 2
PEER_TX = PEER_TT + PEER_AHEAD * PEER_G


def _split_bf16(x):
    hi = x.astype(jnp.bfloat16)
    lo = (x - hi.astype(jnp.float32)).astype(jnp.bfloat16)
    return hi, lo


def _peer_kernel(idx_ref, g_ref, x_ref, tab_ref, sel_ref, exp_ref, mask_ref, out_ref, *scratch):
    bufs = scratch[:PEER_NSLOT]
    sem = scratch[PEER_NSLOT]
    step = pl.program_id(0)
    last = pl.num_programs(0) - 1
    G, HK, R = PEER_G, PEER_HK, PEER_ROWS

    def issue(tok0, slot):
        for j in range(G):
            for k in range(HK):
                e = idx_ref[0, tok0 + j, k]
                pltpu.make_async_copy(tab_ref.at[e], bufs[slot].at[j * HK + k], sem.at[slot]).start(priority=k % 2)

    def wait(slot):
        pltpu.make_async_copy(tab_ref.at[pl.ds(0, G * HK)], bufs[slot], sem.at[slot]).wait()

    mask = mask_ref[...]

    def compute(tok0, slot):
        buf = bufs[slot]
        xs = x_ref[pl.ds(tok0, G)]
        vbs = []
        rms = []
        for j in range(G):
            w = buf[j * HK:(j + 1) * HK].reshape(HK * R, PEER_LANES)
            ub = pltpu.bitcast(w << 16, jnp.float32).astype(jnp.bfloat16)
            vbs.append(pltpu.bitcast(w & jnp.uint32(0xFFFF0000), jnp.float32).astype(jnp.bfloat16))
            xh, xl = _split_bf16(xs[j])
            x16 = jnp.concatenate([xh, xl], axis=0)
            rt = lax.dot_general(x16, ub, (((1,), (1,)), ((), ())), preferred_element_type=jnp.float32)
            rms.append((rt[0:R] + rt[R:]) * mask)
        rm = jnp.concatenate(rms, axis=0)
        rh, rl = _split_bf16(rm)
        z = jnp.dot(jnp.concatenate([rh, rl], axis=0), sel_ref[...], preferred_element_type=jnp.float32)
        z = z[0:G * R] + z[G * R:]
        act = jnp.sum(z.reshape(G, R, HK), axis=1)
        a = jax.nn.gelu(act) * g_ref[pl.ds(tok0, G), :]
        ah, al = _split_bf16(a)
        arep = jnp.dot(jnp.concatenate([ah, al], axis=0), exp_ref[...], preferred_element_type=jnp.float32)
        arep = arep[0:G] + arep[G:]
        for j in range(G):
            am = jnp.broadcast_to(arep[j:j + 1], (R, HK * R)) * mask
            amh, aml = _split_bf16(am)
            o = jnp.dot(jnp.concatenate([amh, aml], axis=0), vbs[j], preferred_element_type=jnp.float32)
            out_ref[tok0 + j] = o[0:R] + o[R:]

    @pl.when(step == 0)
    def _():
        for a in range(PEER_AHEAD):
            issue(a * G, a)

    def body(r, c):
        for s in range(PEER_NSLOT):
            tok0 = pl.multiple_of((r * PEER_NSLOT + s) * G, G)
            wait(s)
            issue(tok0 + PEER_AHEAD * G, (s + PEER_AHEAD) % PEER_NSLOT)
            compute(tok0, s)
        return c

    lax.fori_loop(0, PEER_NG // PEER_NSLOT, body, 0)

    @pl.when(step == last)
    def _():
        for a in range(PEER_AHEAD):
            wait(a)


def _peer_experts(idx, g, x3, table):
    T = idx.shape[0]
    ns = T // PEER_TT
    HK, R = PEER_HK, PEER_ROWS
    r = np.arange(HK * R)
    sel = jnp.asarray(r[:, None] // R == np.arange(HK)[None, :], jnp.bfloat16)
    mask = jnp.asarray(r[None, :] % R == np.arange(R)[:, None], jnp.float32)
    idx3 = idx.reshape(ns, PEER_TT, HK)
    ahead = PEER_AHEAD * PEER_G
    nxt = jnp.concatenate([idx3[1:, :ahead], idx3[-1:, :ahead]], axis=0)
    idx_ext = jnp.concatenate([idx3, nxt], axis=1)
    return pl.pallas_call(
        _peer_kernel,
        grid=(ns,),
        in_specs=[
            pl.BlockSpec((1, PEER_TX, HK), lambda i: (i, 0, 0), memory_space=pltpu.SMEM),
            pl.BlockSpec((PEER_TT, HK), lambda i: (i, 0)),
            pl.BlockSpec((PEER_TT, R, PEER_LANES), lambda i: (i, 0, 0)),
            pl.BlockSpec(memory_space=pl.ANY),
            pl.BlockSpec((HK * R, HK), lambda i: (0, 0)),
            pl.BlockSpec((HK, HK * R), lambda i: (0, 0)),
            pl.BlockSpec((R, HK * R), lambda i: (0, 0)),
        ],
        out_specs=pl.BlockSpec((PEER_TT, R, PEER_LANES), lambda i: (i, 0, 0)),
        out_shape=jax.ShapeDtypeStruct((T, R, PEER_LANES), jnp.float32),
        scratch_shapes=[pltpu.VMEM((PEER_G * HK, R, PEER_LANES), jnp.uint32) for _ in range(PEER_NSLOT)]
        + [pltpu.SemaphoreType.DMA((PEER_NSLOT,))],
        compiler_params=pltpu.CompilerParams(dimension_semantics=("arbitrary",)),
        name="peer_experts",
    )(idx_ext, g, x3, table, sel, sel.T, mask)


def _pack_expert_tables(u, v):
    ub = lax.bitcast_convert_type(u.astype(jnp.bfloat16), jnp.uint16).astype(jnp.uint32)
    vb = lax.bitcast_convert_type(v.astype(jnp.bfloat16), jnp.uint16).astype(jnp.uint32)
    return ((vb << 16) | ub).reshape(u.shape[0], PEER_ROWS, PEER_LANES)


ROUTE_TT = 128
ROUTE_GROUP = 8


def _cand_layout():
    K = PEER_TOPK
    ii, jj, ok = [], [], []

    def add(i_list, j_list, ok_list):
        ii.extend(i_list)
        jj.extend(j_list)
        ok.extend(ok_list)

    add([0] * 8, list(range(8)), [True] * 8)
    add([0] * 8, list(range(8, 16)), [True] * 8)
    for i in range(1, 8):
        jmax = K // (i + 1) - 1
        add([i] * 8, list(range(8)), [j <= jmax for j in range(8)])
    add(list(range(8, 16)), [0] * 8, [True] * 8)
    return np.array(ii), np.array(jj), np.array(ok)


_CAND_I, _CAND_J, _CAND_OK = _cand_layout()
ROUTE_NC = len(_CAND_I)


def _top16(s):
    K, TT = PEER_TOPK, ROUTE_TT
    kidx = lax.broadcasted_iota(jnp.int32, (N_KEYS, TT), 0)
    sub16 = lax.broadcasted_iota(jnp.int32, (K, TT), 0)
    ninf = jnp.float32(-jnp.inf)
    ts = jnp.zeros((K, TT), jnp.float32)
    ti = jnp.zeros((K, TT), jnp.int32)
    for r in range(K):
        m = jnp.max(s, axis=0, keepdims=True)
        imin = jnp.min(jnp.where(s == m, kidx, N_KEYS), axis=0, keepdims=True)
        s = jnp.where(kidx == imin, ninf, s)
        ts = jnp.where(sub16 == r, m, ts)
        ti = jnp.where(sub16 == r, imin, ti)
    return ts, ti


def _head_route(s0, s1, flat, neg):
    K, TT = PEER_TOPK, ROUTE_TT
    sub16 = lax.broadcasted_iota(jnp.int32, (K, TT), 0)
    ninf = jnp.float32(-jnp.inf)
    ts0, ti0 = _top16(s0)
    ts1, ti1 = _top16(s1)
    e0 = ti0 * N_KEYS

    def row(a, i):
        return jnp.broadcast_to(a[i:i + 1, :], (ROUTE_GROUP, TT))

    cv = [row(ts0, 0) + ts1[0:8], row(ts0, 0) + ts1[8:16]]
    ev = [row(e0, 0) + ti1[0:8], row(e0, 0) + ti1[8:16]]
    for i in range(1, 8):
        cv.append(row(ts0, i) + ts1[0:8])
        ev.append(row(e0, i) + ti1[0:8])
    cv.append(ts0[8:16] + row(ts1, 0))
    ev.append(e0[8:16] + row(ti1, 0))
    cand = jnp.concatenate(cv, axis=0) + neg
    eid = jnp.concatenate(ev, axis=0)
    cs = jnp.zeros((K, TT), jnp.float32)
    ei = jnp.zeros((K, TT), jnp.int32)
    m0 = None
    for r in range(K):
        m = jnp.max(cand, axis=0, keepdims=True)
        if r == 0:
            m0 = m
        fmin = jnp.min(jnp.where(cand == m, flat, 4 * K * K), axis=0, keepdims=True)
        hit = flat == fmin
        e = jnp.max(jnp.where(hit, eid, -1), axis=0, keepdims=True)
        cand = jnp.where(hit, ninf, cand)
        cs = jnp.where(sub16 == r, m, cs)
        ei = jnp.where(sub16 == r, e, ei)
    p = jnp.exp(cs - m0)
    g = p / jnp.sum(p, axis=0, keepdims=True)
    return ei, g


def _route_kernel(h_ref, gn_ref, ws_ref, flat_ref, neg_ref, xn_ref, idx_ref, g_ref, s_ref, it_ref, gt_ref):
    x = h_ref[...]
    xn = x * lax.rsqrt(jnp.mean(x * x, axis=-1, keepdims=True) + EPS) * gn_ref[...]
    xn_ref[...] = xn
    s_ref[...] = lax.dot_general(ws_ref[...], xn.astype(jnp.bfloat16), (((1,), (1,)), ((), ())),
                                 preferred_element_type=jnp.float32)
    flat = flat_ref[...]
    neg = neg_ref[...]

    def head(h, c):
        base = pl.multiple_of(h * 2 * N_KEYS, 2 * N_KEYS)
        ei, g = _head_route(s_ref[pl.ds(base, N_KEYS), :], s_ref[pl.ds(base + N_KEYS, N_KEYS), :], flat, neg)
        ob = pl.multiple_of(h * PEER_TOPK, PEER_TOPK)
        it_ref[pl.ds(ob, PEER_TOPK), :] = ei
        gt_ref[pl.ds(ob, PEER_TOPK), :] = g
        return c

    lax.fori_loop(0, PEER_HEADS, head, 0)
    idx_ref[...] = it_ref[...].T
    g_ref[...] = gt_ref[...].T


def _peer_route(h2, g_ffn, ws_t):
    T = h2.shape[0]
    TT, K, NC, HK = ROUTE_TT, PEER_TOPK, ROUTE_NC, PEER_HK
    nrow = PEER_HEADS * 2 * N_KEYS
    flat = jnp.asarray(np.broadcast_to((_CAND_I * K + _CAND_J)[:, None], (NC, TT)), jnp.int32)
    neg = jnp.asarray(np.broadcast_to(np.where(_CAND_OK, 0.0, -np.inf)[:, None], (NC, TT)), jnp.float32)
    return pl.pallas_call(
        _route_kernel,
        grid=(T // TT,),
        in_specs=[
            pl.BlockSpec((TT, D_MODEL), lambda i: (i, 0)),
            pl.BlockSpec((1, D_MODEL), lambda i: (0, 0)),
            pl.BlockSpec((nrow, D_MODEL), lambda i: (0, 0)),
            pl.BlockSpec((NC, TT), lambda i: (0, 0)),
            pl.BlockSpec((NC, TT), lambda i: (0, 0)),
        ],
        out_specs=[
            pl.BlockSpec((TT, D_MODEL), lambda i: (i, 0)),
            pl.BlockSpec((TT, HK), lambda i: (i, 0)),
            pl.BlockSpec((TT, HK), lambda i: (i, 0)),
        ],
        out_shape=[
            jax.ShapeDtypeStruct((T, D_MODEL), jnp.float32),
            jax.ShapeDtypeStruct((T, HK), jnp.int32),
            jax.ShapeDtypeStruct((T, HK), jnp.float32),
        ],
        scratch_shapes=[pltpu.VMEM((nrow, TT), jnp.float32),
                        pltpu.VMEM((HK, TT), jnp.int32),
                        pltpu.VMEM((HK, TT), jnp.float32)],
        compiler_params=pltpu.CompilerParams(dimension_semantics=("arbitrary",)),
        name="peer_route",
    )(h2, g_ffn.reshape(1, D_MODEL), ws_t, flat, neg)


def _score_weights(wq, keys):
    n = PEER_HEADS * 2
    eye = jnp.eye(n, dtype=keys.dtype)
    kd = keys.reshape(n, N_KEYS, D_HALF)
    block_diag = (eye[:, None, :, None] * kd[:, :, None, :]).reshape(n * N_KEYS, n * D_HALF)
    return _matmul(block_diag, wq.T, tm=512, tn=512).astype(jnp.bfloat16)


def _peer(h2, g_ffn, wq, keys, u_tab, v_tab):
    T = h2.shape[0]
    xn, experts, g = _peer_route(h2, g_ffn, _score_weights(wq, keys))
    out = _peer_experts(experts, g, xn.reshape(T, PEER_ROWS, PEER_LANES), _pack_expert_tables(u_tab, v_tab))
    return out.reshape(T, D_MODEL)


def kernel(x, meta, g_final, g_mix, w_in, b_gate, hy_conv_w, hy_conv_b, hy_w1, hy_b1, hy_w2, hy_b2,
           hy_w3, hy_b3, hy_freq, hy_skip, hy_out, rg_conv_w, rg_conv_b, rg_gate_w, rg_gate_b,
           rg_lambda, rg_out, w_out, g_ffn, peer_wq, peer_keys, peer_u, peer_v):
    B = x.shape[0]
    h = jnp.concatenate([jnp.broadcast_to(meta[None], (B, N_META, D_MODEL)), x], axis=1)
    L = h.shape[1]
    s_hy = 3 * D_HY
    s_rx = s_hy + D_RG
    s_rg = s_rx + D_RG
    l = 0
    n = _rmsnorm(h, g_mix[l])
    proj = _matmul(n.reshape(B * L, D_MODEL), w_in[l]).reshape(B, L, -1)
    y_hy = _hyena_branch(proj[..., :s_hy], hy_conv_w[l], hy_conv_b[l], hy_w1[l], hy_b1[l], hy_w2[l],
                         hy_b2[l], hy_w3[l], hy_b3[l], hy_freq[l], hy_skip[l], hy_out[l])
    y_rg = _rglru_branch(proj[..., s_hy:s_rx], proj[..., s_rx:s_rg], rg_conv_w[l], rg_conv_b[l],
                         rg_gate_w[l], rg_gate_b[l], rg_lambda[l], rg_out[l])
    gl = proj[..., s_rg:] + b_gate[l]
    merged = jax.nn.sigmoid(gl[..., :D_MODEL]) * y_hy + jax.nn.sigmoid(gl[..., D_MODEL:]) * y_rg
    h = h + _matmul(merged.reshape(B * L, D_MODEL), w_out[l]).reshape(B, L, D_MODEL)
    h2 = h.reshape(B * L, D_MODEL)
    h = (h2 + _peer(h2, g_ffn[l], peer_wq[l], peer_keys[l], peer_u[l], peer_v[l])).reshape(B, L, D_MODEL)
    h = _rmsnorm(h, g_final)
    return h[:, N_META:]
```

```python
import functools
import math

import jax
import jax.numpy as jnp
import numpy as np
from jax import lax
from jax.experimental import pallas as pl
from jax.experimental.pallas import tpu as pltpu

D_MODEL = 1024
N_META = 16
EPS = 1e-6
D_HY = 512
HY_EMB = 33
HY_BANDS = (HY_EMB - 1) // 2
HY_TARGET = 1e-2
HY_DECAY_HI = 0.3
HY_DECAY_LO = 1.5
D_RG = 1024
RG_HEADS = 8
RG_HEAD_DIM = D_RG // RG_HEADS
RG_CONV = 4
RG_C = 8.0
PEER_HEADS = 8
N_KEYS = 128
D_KEY = 256
D_HALF = D_KEY // 2
PEER_TOPK = 16
PEER_CHUNK = 16


def _mm_kernel(a_ref, b_ref, o_ref):
    o_ref[...] = jnp.dot(a_ref[...].astype(jnp.bfloat16), b_ref[...],
                         preferred_element_type=jnp.float32)


def _matmul(a, b, tm=512, tn=512):
    m, k = a.shape
    _, n = b.shape
    tn = min(tn, n)
    return pl.pallas_call(
        _mm_kernel,
        grid=(pl.cdiv(m, tm), n // tn),
        in_specs=[pl.BlockSpec((tm, k), lambda i, j: (i, 0)),
                  pl.BlockSpec((k, tn), lambda i, j: (0, j))],
        out_specs=pl.BlockSpec((tm, tn), lambda i, j: (i, j)),
        out_shape=jax.ShapeDtypeStruct((m, n), jnp.float32),
    )(a, b.astype(jnp.bfloat16))


def _rmsnorm(x, g):
    y = x * lax.rsqrt(jnp.mean(x * x, axis=-1, keepdims=True) + EPS)
    return y * g


def _hyena_filter(L, w1, b1, w2, b2, w3, b3, freq):
    f32 = jnp.float32
    t = jnp.linspace(0.0, 1.0, L, dtype=f32)[:, None]
    w = (2.0 * math.pi / L) * jnp.arange(L, dtype=f32)[:, None]
    bands = jnp.linspace(1e-4, HY_BANDS - 1, HY_BANDS, dtype=f32)[None, :]
    z = jnp.concatenate([t, jnp.cos(bands * w), -jnp.sin(bands * w)], axis=-1)
    hdn = jnp.sin(freq[0] * (z @ w1 + b1))
    hdn = jnp.sin(freq[1] * (hdn @ w2 + b2))
    k = (hdn @ w3 + b3).reshape(L, 2, D_HY)
    deltas = jnp.abs(jnp.linspace(math.log(HY_TARGET) / HY_DECAY_HI,
                                  math.log(HY_TARGET) / HY_DECAY_LO, D_HY, dtype=f32))
    k = k * jnp.exp(-t * deltas)[:, None, :]
    return k[:, 0], k[:, 1]


HY_P = 256


def _longconv_kernel(kk_ref, u_ref, y_ref, acc_ref, *, nblk, nb):
    P = HY_P
    npad = nblk * P
    acc_ref[...] = jnp.zeros_like(acc_ref)
    for d in range(-(nblk - 1), nblk):
        start = npad + d * P - P
        w = kk_ref[0, :, start:start + 2 * P]
        m = pltpu.roll(jnp.broadcast_to(w, (P, 2 * P)), P, 1, stride=1, stride_axis=0)[:, :P]
        m = m.astype(jnp.bfloat16)
        if d >= 0:
            lhs = u_ref[0, 0:(nblk - d) * nb, :]
            acc_ref[d * nb:nblk * nb, :] += jnp.dot(lhs, m, preferred_element_type=jnp.float32)
        else:
            lhs = u_ref[0, (-d) * nb:nblk * nb, :]
            acc_ref[0:(nblk + d) * nb, :] += jnp.dot(lhs, m, preferred_element_type=jnp.float32)
    y_ref[0] = acc_ref[...]


def _long_conv(u, k_fwd, k_bwd):
    Bb, L, C = u.shape
    P = HY_P
    nblk = -(-L // P)
    npad = nblk * P
    kk = jnp.concatenate([jnp.zeros((npad - (L - 1), C), jnp.float32), k_bwd[:0:-1], k_fwd,
                          jnp.zeros((npad - L, C), jnp.float32)], axis=0)
    kk = kk.T.reshape(C, 1, 2 * npad)
    up = jnp.pad(u, ((0, 0), (0, npad - L), (0, 0))).astype(jnp.bfloat16)
    u2 = up.reshape(Bb, nblk, P, C).transpose(3, 1, 0, 2).reshape(C, nblk * Bb, P)
    y2 = pl.pallas_call(
        functools.partial(_longconv_kernel, nblk=nblk, nb=Bb),
        grid=(C,),
        in_specs=[pl.BlockSpec((1, 1, 2 * npad), lambda c: (c, 0, 0)),
                  pl.BlockSpec((1, nblk * Bb, P), lambda c: (c, 0, 0))],
        out_specs=pl.BlockSpec((1, nblk * Bb, P), lambda c: (c, 0, 0)),
        out_shape=jax.ShapeDtypeStruct((C, nblk * Bb, P), jnp.float32),
        scratch_shapes=[pltpu.VMEM((nblk * Bb, P), jnp.float32)],
        compiler_params=pltpu.CompilerParams(dimension_semantics=("arbitrary",)),
        name="hyena_longconv",
    )(kk, u2)
    y = y2.reshape(C, nblk, Bb, P).transpose(2, 1, 3, 0).reshape(Bb, npad, C)
    return y[:, :L]


def _hyena_branch(hy_in, conv_w, conv_b, w1, b1, w2, b2, w3, b3, freq, skip, w_proj):
    B, L, _ = hy_in.shape
    hp = jnp.pad(hy_in, ((0, 0), (1, 1), (0, 0)))
    u = hp[:, 0:L] * conv_w[0] + hp[:, 1:L + 1] * conv_w[1] + hp[:, 2:L + 2] * conv_w[2] + conv_b
    x0, x1, v = u[..., :D_HY], u[..., D_HY:2 * D_HY], u[..., 2 * D_HY:]
    k_fwd, k_bwd = _hyena_filter(L, w1, b1, w2, b2, w3, b3, freq)
    vx = v * x1
    y = (_long_conv(vx, k_fwd, k_bwd) + vx * skip) * x0
    return _matmul(y.reshape(B * L, D_HY), w_proj).reshape(B, L, D_MODEL)


RG_TL = 257


def _rglru_kernel(x_ref, cw_ref, cb_ref, gw_ref, gb_ref, lam_ref, o_ref, halo_ref, h_ref, a_ref, b_ref,
                  *, reverse, nb):
    c = pl.program_id(1)
    rows = x_ref.shape[0]
    tl = rows // nb
    hr = (RG_CONV - 1) * nb

    @pl.when(c == 0)
    def _():
        halo_ref[...] = jnp.zeros_like(halo_ref)
        h_ref[...] = jnp.zeros_like(h_ref)

    x = x_ref[...]
    if reverse:
        xe = jnp.concatenate([x, halo_ref[...]], axis=0)
        xc = cb_ref[...] + sum(xe[j * nb:j * nb + rows] * cw_ref[j:j + 1, :] for j in range(RG_CONV))
        halo_ref[...] = x[0:hr]
    else:
        xe = jnp.concatenate([halo_ref[...], x], axis=0)
        xc = cb_ref[...] + sum(xe[hr - j * nb:hr - j * nb + rows] * cw_ref[j:j + 1, :] for j in range(RG_CONV))
        halo_ref[...] = x[rows - hr:rows]
    xb = xc.astype(jnp.bfloat16)
    gr = jnp.dot(xb, gw_ref[0, 0], preferred_element_type=jnp.float32) + gb_ref[0:1, :]
    gi = jnp.dot(xb, gw_ref[1, 0], preferred_element_type=jnp.float32) + gb_ref[1:2, :]
    r = jax.nn.sigmoid(gr)
    i = jax.nn.sigmoid(gi)
    log_a = -RG_C * r * jax.nn.softplus(-lam_ref[...])
    a = jnp.exp(log_a)
    mult = jnp.sqrt(1.0 - jnp.exp(2.0 * log_a))
    row = lax.broadcasted_iota(jnp.int32, (rows, RG_HEAD_DIM), 0)
    first_rows = (row >= rows - nb) if reverse else (row < nb)
    mult = jnp.where(jnp.logical_and(first_rows, c == 0), 1.0, mult)
    a_ref[...] = a
    b_ref[...] = mult * i * xc

    def step(t, h):
        tt = (tl - 1 - t) if reverse else t
        r0 = pl.multiple_of(tt * nb, nb)
        h = a_ref[pl.ds(r0, nb), :] * h + b_ref[pl.ds(r0, nb), :]
        o_ref[pl.ds(r0, nb), :] = h
        return h

    h_ref[...] = lax.fori_loop(0, tl, step, h_ref[...], unroll=8)


def _rglru_direction(xt, nb, conv_w, conv_b, gate_w, gate_b, lam, reverse):
    T = xt.shape[0]
    L = T // nb
    assert L % RG_TL == 0 and nb % 8 == 0
    nch = L // RG_TL
    rows = RG_TL * nb
    cmap = (lambda c: nch - 1 - c) if reverse else (lambda c: c)
    return pl.pallas_call(
        functools.partial(_rglru_kernel, reverse=reverse, nb=nb),
        grid=(RG_HEADS, nch),
        in_specs=[
            pl.BlockSpec((rows, RG_HEAD_DIM), lambda h, c: (cmap(c), h)),
            pl.BlockSpec((RG_CONV, RG_HEAD_DIM), lambda h, c: (0, h)),
            pl.BlockSpec((1, RG_HEAD_DIM), lambda h, c: (0, h)),
            pl.BlockSpec((2, 1, RG_HEAD_DIM, RG_HEAD_DIM), lambda h, c: (0, h, 0, 0)),
            pl.BlockSpec((2, RG_HEAD_DIM), lambda h, c: (0, h)),
            pl.BlockSpec((1, RG_HEAD_DIM), lambda h, c: (0, h)),
        ],
        out_specs=pl.BlockSpec((rows, RG_HEAD_DIM), lambda h, c: (cmap(c), h)),
        out_shape=jax.ShapeDtypeStruct((T, D_RG), jnp.float32),
        scratch_shapes=[pltpu.VMEM(((RG_CONV - 1) * nb, RG_HEAD_DIM), jnp.float32),
                        pltpu.VMEM((nb, RG_HEAD_DIM), jnp.float32),
                        pltpu.VMEM((rows, RG_HEAD_DIM), jnp.float32),
                        pltpu.VMEM((rows, RG_HEAD_DIM), jnp.float32)],
        compiler_params=pltpu.CompilerParams(dimension_semantics=("arbitrary", "arbitrary")),
        name="rglru_bwd" if reverse else "rglru_fwd",
    )(xt, conv_w, conv_b.reshape(1, D_RG), gate_w.astype(jnp.bfloat16), gate_b, lam.reshape(1, D_RG))


def _rglru_branch(rg_x, rg_g, conv_w, conv_b, gate_w, gate_b, lam, w_proj):
    B, L, _ = rg_x.shape
    xt = rg_x.transpose(1, 0, 2).reshape(L * B, D_RG)
    h_f = _rglru_direction(xt, B, conv_w[0], conv_b[0], gate_w[0], gate_b[0], lam[0], False)
    h_b = _rglru_direction(xt, B, conv_w[1], conv_b[1], gate_w[1], gate_b[1], lam[1], True)
    h = (h_f + h_b).reshape(L, B, D_RG).transpose(1, 0, 2)
    y = h * jax.nn.gelu(rg_g)
    return _matmul(y.reshape(B * L, D_RG), w_proj).reshape(B, L, D_MODEL)


PEER_HK = PEER_HEADS * PEER_TOPK
PEER_LANES = 128
PEER_ROWS = D_MODEL // PEER_LANES
PEER_TT = 128
PEER_G = 8
PEER_NG = PEER_TT // PEER_G
PEER_NSLOT = 4
PEER_AHEAD = 2
PEER_TX = PEER_TT + PEER_AHEAD * PEER_G


def _split_bf16(x):
    hi = x.astype(jnp.bfloat16)
    lo = (x - hi.astype(jnp.float32)).astype(jnp.bfloat16)
    return hi, lo


def _peer_kernel(idx_ref, g_ref, x_ref, tab_ref, sel_ref, exp_ref, mask_ref, out_ref, *scratch):
    bufs = scratch[:PEER_NSLOT]
    sem = scratch[PEER_NSLOT]
    step = pl.program_id(0)
    last = pl.num_programs(0) - 1
    G, HK, R = PEER_G, PEER_HK, PEER_ROWS

    def issue(tok0, slot):
        for j in range(G):
            for k in range(HK):
                e = idx_ref[0, tok0 + j, k]
                pltpu.make_async_copy(tab_ref.at[e], bufs[slot].at[j * HK + k], sem.at[slot]).start(priority=k % 2)

    def wait(slot):
        pltpu.make_async_copy(tab_ref.at[pl.ds(0, G * HK)], bufs[slot], sem.at[slot]).wait()

    mask = mask_ref[...]

    def compute(tok0, slot):
        buf = bufs[slot]
        xs = x_ref[pl.ds(tok0, G)]
        vbs = []
        rms = []
        for j in range(G):
            w = buf[j * HK:(j + 1) * HK].reshape(HK * R, PEER_LANES)
            ub = pltpu.bitcast(w << 16, jnp.float32).astype(jnp.bfloat16)
            vbs.append(pltpu.bitcast(w & jnp.uint32(0xFFFF0000), jnp.float32).astype(jnp.bfloat16))
            xh, xl = _split_bf16(xs[j])
            x16 = jnp.concatenate([xh, xl], axis=0)
            rt = lax.dot_general(x16, ub, (((1,), (1,)), ((), ())), preferred_element_type=jnp.float32)
            rms.append((rt[0:R] + rt[R:]) * mask)
        rm = jnp.concatenate(rms, axis=0)
        rh, rl = _split_bf16(rm)
        z = jnp.dot(jnp.concatenate([rh, rl], axis=0), sel_ref[...], preferred_element_type=jnp.float32)
        z = z[0:G * R] + z[G * R:]
        act = jnp.sum(z.reshape(G, R, HK), axis=1)
        a = jax.nn.gelu(act) * g_ref[pl.ds(tok0, G), :]
        ah, al = _split_bf16(a)
        arep = jnp.dot(jnp.concatenate([ah, al], axis=0), exp_ref[...], preferred_element_type=jnp.float32)
        arep = arep[0:G] + arep[G:]
        for j in range(G):
            am = jnp.broadcast_to(arep[j:j + 1], (R, HK * R)) * mask
            amh, aml = _split_bf16(am)
            o = jnp.dot(jnp.concatenate([amh, aml], axis=0), vbs[j], preferred_element_type=jnp.float32)
            out_ref[tok0 + j] = o[0:R] + o[R:]

    @pl.when(step == 0)
    def _():
        for a in range(PEER_AHEAD):
            issue(a * G, a)

    def body(r, c):
        for s in range(PEER_NSLOT):
            tok0 = pl.multiple_of((r * PEER_NSLOT + s) * G, G)
            wait(s)
            issue(tok0 + PEER_AHEAD * G, (s + PEER_AHEAD) % PEER_NSLOT)
            compute(tok0, s)
        return c

    lax.fori_loop(0, PEER_NG // PEER_NSLOT, body, 0)

    @pl.when(step == last)
    def _():
        for a in range(PEER_AHEAD):
            wait(a)


def _peer_experts(idx, g, x3, table):
    T = idx.shape[0]
    ns = T // PEER_TT
    HK, R = PEER_HK, PEER_ROWS
    r = np.arange(HK * R)
    sel = jnp.asarray(r[:, None] // R == np.arange(HK)[None, :], jnp.bfloat16)
    mask = jnp.asarray(r[None, :] % R == np.arange(R)[:, None], jnp.float32)
    idx3 = idx.reshape(ns, PEER_TT, HK)
    ahead = PEER_AHEAD * PEER_G
    nxt = jnp.concatenate([idx3[1:, :ahead], idx3[-1:, :ahead]], axis=0)
    idx_ext = jnp.concatenate([idx3, nxt], axis=1)
    return pl.pallas_call(
        _peer_kernel,
        grid=(ns,),
        in_specs=[
            pl.BlockSpec((1, PEER_TX, HK), lambda i: (i, 0, 0), memory_space=pltpu.SMEM),
            pl.BlockSpec((PEER_TT, HK), lambda i: (i, 0)),
            pl.BlockSpec((PEER_TT, R, PEER_LANES), lambda i: (i, 0, 0)),
            pl.BlockSpec(memory_space=pl.ANY),
            pl.BlockSpec((HK * R, HK), lambda i: (0, 0)),
            pl.BlockSpec((HK, HK * R), lambda i: (0, 0)),
            pl.BlockSpec((R, HK * R), lambda i: (0, 0)),
        ],
        out_specs=pl.BlockSpec((PEER_TT, R, PEER_LANES), lambda i: (i, 0, 0)),
        out_shape=jax.ShapeDtypeStruct((T, R, PEER_LANES), jnp.float32),
        scratch_shapes=[pltpu.VMEM((PEER_G * HK, R, PEER_LANES), jnp.uint32) for _ in range(PEER_NSLOT)]
        + [pltpu.SemaphoreType.DMA((PEER_NSLOT,))],
        compiler_params=pltpu.CompilerParams(dimension_semantics=("arbitrary",)),
        name="peer_experts",
    )(idx_ext, g, x3, table, sel, sel.T, mask)


def _pack_expert_tables(u, v):
    ub = lax.bitcast_convert_type(u.astype(jnp.bfloat16), jnp.uint16).astype(jnp.uint32)
    vb = lax.bitcast_convert_type(v.astype(jnp.bfloat16), jnp.uint16).astype(jnp.uint32)
    return ((vb << 16) | ub).reshape(u.shape[0], PEER_ROWS, PEER_LANES)


ROUTE_TT = 128
ROUTE_GROUP = 8


def _cand_layout():
    K = PEER_TOPK
    ii, jj, ok = [], [], []

    def add(i_list, j_list, ok_list):
        ii.extend(i_list)
        jj.extend(j_list)
        ok.extend(ok_list)

    add([0] * 8, list(range(8)), [True] * 8)
    add([0] * 8, list(range(8, 16)), [True] * 8)
    for i in range(1, 8):
        jmax = K // (i + 1) - 1
        add([i] * 8, list(range(8)), [j <= jmax for j in range(8)])
    add(list(range(8, 16)), [0] * 8, [True] * 8)
    return np.array(ii), np.array(jj), np.array(ok)


_CAND_I, _CAND_J, _CAND_OK = _cand_layout()
ROUTE_NC = len(_CAND_I)


def _top16(s):
    K, TT = PEER_TOPK, ROUTE_TT
    kidx = lax.broadcasted_iota(jnp.int32, (N_KEYS, TT), 0)
    sub16 = lax.broadcasted_iota(jnp.int32, (K, TT), 0)
    ninf = jnp.float32(-jnp.inf)
    ts = jnp.zeros((K, TT), jnp.float32)
    ti = jnp.zeros((K, TT), jnp.int32)
    for r in range(K):
        m = jnp.max(s, axis=0, keepdims=True)
        imin = jnp.min(jnp.where(s == m, kidx, N_KEYS), axis=0, keepdims=True)
        s = jnp.where(kidx == imin, ninf, s)
        ts = jnp.where(sub16 == r, m, ts)
        ti = jnp.where(sub16 == r, imin, ti)
    return ts, ti


def _head_route(s0, s1, flat, neg):
    K, TT = PEER_TOPK, ROUTE_TT
    sub16 = lax.broadcasted_iota(jnp.int32, (K, TT), 0)
    ninf = jnp.float32(-jnp.inf)
    ts0, ti0 = _top16(s0)
    ts1, ti1 = _top16(s1)
    e0 = ti0 * N_KEYS

    def row(a, i):
        return jnp.broadcast_to(a[i:i + 1, :], (ROUTE_GROUP, TT))

    cv = [row(ts0, 0) + ts1[0:8], row(ts0, 0) + ts1[8:16]]
    ev = [row(e0, 0) + ti1[0:8], row(e0, 0) + ti1[8:16]]
    for i in range(1, 8):
        cv.append(row(ts0, i) + ts1[0:8])
        ev.append(row(e0, i) + ti1[0:8])
    cv.append(ts0[8:16] + row(ts1, 0))
    ev.append(e0[8:16] + row(ti1, 0))
    cand = jnp.concatenate(cv, axis=0) + neg
    eid = jnp.concatenate(ev, axis=0)
    cs = jnp.zeros((K, TT), jnp.float32)
    ei = jnp.zeros((K, TT), jnp.int32)
    m0 = None
    for r in range(K):
        m = jnp.max(cand, axis=0, keepdims=True)
        if r == 0:
            m0 = m
        fmin = jnp.min(jnp.where(cand == m, flat, 4 * K * K), axis=0, keepdims=True)
        hit = flat == fmin
        e = jnp.max(jnp.where(hit, eid, -1), axis=0, keepdims=True)
        cand = jnp.where(hit, ninf, cand)
        cs = jnp.where(sub16 == r, m, cs)
        ei = jnp.where(sub16 == r, e, ei)
    p = jnp.exp(cs - m0)
    g = p / jnp.sum(p, axis=0, keepdims=True)
    return ei, g


def _route_kernel(h_ref, gn_ref, ws_ref, flat_ref, neg_ref, xn_ref, idx_ref, g_ref, s_ref, it_ref, gt_ref):
    x = h_ref[...]
    xn = x * lax.rsqrt(jnp.mean(x * x, axis=-1, keepdims=True) + EPS) * gn_ref[...]
    xn_ref[...] = xn
    s_ref[...] = lax.dot_general(ws_ref[...], xn.astype(jnp.bfloat16), (((1,), (1,)), ((), ())),
                                 preferred_element_type=jnp.float32)
    flat = flat_ref[...]
    neg = neg_ref[...]

    def head(h, c):
        base = pl.multiple_of(h * 2 * N_KEYS, 2 * N_KEYS)
        ei, g = _head_route(s_ref[pl.ds(base, N_KEYS), :], s_ref[pl.ds(base + N_KEYS, N_KEYS), :], flat, neg)
        ob = pl.multiple_of(h * PEER_TOPK, PEER_TOPK)
        it_ref[pl.ds(ob, PEER_TOPK), :] = ei
        gt_ref[pl.ds(ob, PEER_TOPK), :] = g
        return c

    lax.fori_loop(0, PEER_HEADS, head, 0)
    idx_ref[...] = it_ref[...].T
    g_ref[...] = gt_ref[...].T


def _peer_route(h2, g_ffn, ws_t):
    T = h2.shape[0]
    TT, K, NC, HK = ROUTE_TT, PEER_TOPK, ROUTE_NC, PEER_HK
    nrow = PEER_HEADS * 2 * N_KEYS
    flat = jnp.asarray(np.broadcast_to((_CAND_I * K + _CAND_J)[:, None], (NC, TT)), jnp.int32)
    neg = jnp.asarray(np.broadcast_to(np.where(_CAND_OK, 0.0, -np.inf)[:, None], (NC, TT)), jnp.float32)
    return pl.pallas_call(
        _route_kernel,
        grid=(T // TT,),
        in_specs=[
            pl.BlockSpec((TT, D_MODEL), lambda i: (i, 0)),
            pl.BlockSpec((1, D_MODEL), lambda i: (0, 0)),
            pl.BlockSpec((nrow, D_MODEL), lambda i: (0, 0)),
            pl.BlockSpec((NC, TT), lambda i: (0, 0)),
            pl.BlockSpec((NC, TT), lambda i: (0, 0)),
        ],
        out_specs=[
            pl.BlockSpec((TT, D_MODEL), lambda i: (i, 0)),
            pl.BlockSpec((TT, HK), lambda i: (i, 0)),
            pl.BlockSpec((TT, HK), lambda i: (i, 0)),
        ],
        out_shape=[
            jax.ShapeDtypeStruct((T, D_MODEL), jnp.float32),
            jax.ShapeDtypeStruct((T, HK), jnp.int32),
            jax.ShapeDtypeStruct((T, HK), jnp.float32),
        ],
        scratch_shapes=[pltpu.VMEM((nrow, TT), jnp.float32),
                        pltpu.VMEM((HK, TT), jnp.int32),
                        pltpu.VMEM((HK, TT), jnp.float32)],
        compiler_params=pltpu.CompilerParams(dimension_semantics=("arbitrary",)),
        name="peer_route",
    )(h2, g_ffn.reshape(1, D_MODEL), ws_t, flat, neg)


def _score_weights(wq, keys):
    n = PEER_HEADS * 2
    eye = jnp.eye(n, dtype=keys.dtype)
    kd = keys.reshape(n, N_KEYS, D_HALF)
    block_diag = (eye[:, None, :, None] * kd[:, :, None, :]).reshape(n * N_KEYS, n * D_HALF)
    return _matmul(block_diag, wq.T, tm=512, tn=512).astype(jnp.bfloat16)


def _peer(h2, g_ffn, wq, keys, u_tab, v_tab):
    T = h2.shape[0]
    xn, experts, g = _peer_route(h2, g_ffn, _score_weights(wq, keys))
    out = _peer_experts(experts, g, xn.reshape(T, PEER_ROWS, PEER_LANES), _pack_expert_tables(u_tab, v_tab))
    return out.reshape(T, D_MODEL)


def kernel(x, meta, g_final, g_mix, w_in, b_gate, hy_conv_w, hy_conv_b, hy_w1, hy_b1, hy_w2, hy_b2,
           hy_w3, hy_b3, hy_freq, hy_skip, hy_out, rg_conv_w, rg_conv_b, rg_gate_w, rg_gate_b,
           rg_lambda, rg_out, w_out, g_ffn, peer_wq, peer_keys, peer_u, peer_v):
    B = x.shape[0]
    h = jnp.concatenate([jnp.broadcast_to(meta[None], (B, N_META, D_MODEL)), x], axis=1)
    L = h.shape[1]
    s_hy = 3 * D_HY
    s_rx = s_hy + D_RG
    s_rg = s_rx + D_RG
    l = 0
    n = _rmsnorm(h, g_mix[l])
    proj = _matmul(n.reshape(B * L, D_MODEL), w_in[l]).reshape(B, L, -1)
    y_hy = _hyena_branch(proj[..., :s_hy], hy_conv_w[l], hy_conv_b[l], hy_w1[l], hy_b1[l], hy_w2[l],
                         hy_b2[l], hy_w3[l], hy_b3[l], hy_freq[l], hy_skip[l], hy_out[l])
    y_rg = _rglru_branch(proj[..., s_hy:s_rx], proj[..., s_rx:s_rg], rg_conv_w[l], rg_conv_b[l],
                         rg_gate_w[l], rg_gate_b[l], rg_lambda[l], rg_out[l])
    gl = proj[..., s_rg:] + b_gate[l]
    merged = jax.nn.sigmoid(gl[..., :D_MODEL]) * y_hy + jax.nn.sigmoid(gl[..., D_MODEL:]) * y_rg
    h = h + _matmul(merged.reshape(B * L, D_MODEL), w_out[l]).reshape(B, L, D_MODEL)
    h2 = h.reshape(B * L, D_MODEL)
    h = (h2 + _peer(h2, g_ffn[l], peer_wq[l], peer_keys[l], peer_u[l], peer_v[l])).reshape(B, L, D_MODEL)
    h = _rmsnorm(h, g_final)
    return h[:, N_META:]
```

```python
import functools
import math

import jax
import jax.numpy as jnp
import numpy as np
from jax import lax
from jax.experimental import pallas as pl
from jax.experimental.pallas import tpu as pltpu

D_MODEL = 1024
N_META = 16
EPS = 1e-6
D_HY = 512
HY_EMB = 33
HY_BANDS = (HY_EMB - 1) // 2
HY_TARGET = 1e-2
HY_DECAY_HI = 0.3
HY_DECAY_LO = 1.5
D_RG = 1024
RG_HEADS = 8
RG_HEAD_DIM = D_RG // RG_HEADS
RG_CONV = 4
RG_C = 8.0
PEER_HEADS = 8
N_KEYS = 128
D_KEY = 256
D_HALF = D_KEY // 2
PEER_TOPK = 16
PEER_CHUNK = 16


def _mm_kernel(a_ref, b_ref, o_ref):
    o_ref[...] = jnp.dot(a_ref[...].astype(jnp.bfloat16), b_ref[...],
                         preferred_element_type=jnp.float32)


def _matmul(a, b, tm=512, tn=512):
    m, k = a.shape
    _, n = b.shape
    tn = min(tn, n)
    return pl.pallas_call(
        _mm_kernel,
        grid=(pl.cdiv(m, tm), n // tn),
        in_specs=[pl.BlockSpec((tm, k), lambda i, j: (i, 0)),
                  pl.BlockSpec((k, tn), lambda i, j: (0, j))],
        out_specs=pl.BlockSpec((tm, tn), lambda i, j: (i, j)),
        out_shape=jax.ShapeDtypeStruct((m, n), jnp.float32),
    )(a, b.astype(jnp.bfloat16))


MM_TM = 512
MM_TK = 512
MM_VMEM_BYTES = 48 * 1024 * 1024


def _rms(x, g):
    return x * lax.rsqrt(jnp.mean(x * x, axis=-1, keepdims=True) + EPS) * g


def _norm_mm_kernel(a_ref, g_ref, b_ref, o_ref, n_ref):
    @pl.when(pl.program_id(1) == 0)
    def _():
        n_ref[...] = _rms(a_ref[...], g_ref[...]).astype(jnp.bfloat16)

    o_ref[...] = jnp.dot(n_ref[...], b_ref[...], preferred_element_type=jnp.float32)


def _norm_matmul(a, g, b, tn=512):
    m, k = a.shape
    n = b.shape[1]
    return pl.pallas_call(
        _norm_mm_kernel,
        grid=(pl.cdiv(m, MM_TM), n // tn),
        in_specs=[pl.BlockSpec((MM_TM, k), lambda i, j: (i, 0)),
                  pl.BlockSpec((1, k), lambda i, j: (0, 0)),
                  pl.BlockSpec((k, tn), lambda i, j: (0, j))],
        out_specs=pl.BlockSpec((MM_TM, tn), lambda i, j: (i, j)),
        out_shape=jax.ShapeDtypeStruct((m, n), jnp.float32),
        scratch_shapes=[pltpu.VMEM((MM_TM, k), jnp.bfloat16)],
        compiler_params=pltpu.CompilerParams(dimension_semantics=("arbitrary", "arbitrary"),
                                             vmem_limit_bytes=MM_VMEM_BYTES),
        name="norm_in_proj",
    )(a, g.reshape(1, k), b.astype(jnp.bfloat16))


def _gelu_mm_kernel(h_ref, gate_ref, w_ref, o_ref):
    y = h_ref[...] * jax.nn.gelu(gate_ref[...])
    d = jnp.dot(y.astype(jnp.bfloat16), w_ref[...], preferred_element_type=jnp.float32)

    @pl.when(pl.program_id(1) == 0)
    def _():
        o_ref[...] = d

    @pl.when(pl.program_id(1) != 0)
    def _():
        o_ref[...] += d


def _gelu_gate_matmul(hsum, proj, gate_col, w):
    m, k = hsum.shape
    n = w.shape[1]
    assert gate_col % MM_TK == 0 and k % MM_TK == 0
    gb = gate_col // MM_TK
    return pl.pallas_call(
        _gelu_mm_kernel,
        grid=(pl.cdiv(m, MM_TM), k // MM_TK),
        in_specs=[pl.BlockSpec((MM_TM, MM_TK), lambda i, kk: (i, kk)),
                  pl.BlockSpec((MM_TM, MM_TK), lambda i, kk: (i, gb + kk)),
                  pl.BlockSpec((MM_TK, n), lambda i, kk: (kk, 0))],
        out_specs=pl.BlockSpec((MM_TM, n), lambda i, kk: (i, 0)),
        out_shape=jax.ShapeDtypeStruct((m, n), jnp.float32),
        compiler_params=pltpu.CompilerParams(dimension_semantics=("arbitrary", "arbitrary"),
                                             vmem_limit_bytes=MM_VMEM_BYTES),
        name="rg_out_proj",
    )(hsum, proj, w.astype(jnp.bfloat16))


def _merge_mm_kernel(ga_ref, gb_ref, ba_ref, bb_ref, ya_ref, yb_ref, w_ref, h_ref, o_ref):
    m = (jax.nn.sigmoid(ga_ref[...] + ba_ref[...]) * ya_ref[...]
         + jax.nn.sigmoid(gb_ref[...] + bb_ref[...]) * yb_ref[...])
    d = jnp.dot(m.astype(jnp.bfloat16), w_ref[...], preferred_element_type=jnp.float32)

    @pl.when(pl.program_id(1) == 0)
    def _():
        o_ref[...] = h_ref[...] + d

    @pl.when(pl.program_id(1) != 0)
    def _():
        o_ref[...] += d


def _merge_matmul(proj, gate_col, b_gate, y_a, y_b, w, h):
    m, k = y_a.shape
    n = w.shape[1]
    assert gate_col % MM_TK == 0 and k % MM_TK == 0
    nk = k // MM_TK
    ga0 = gate_col // MM_TK
    return pl.pallas_call(
        _merge_mm_kernel,
        grid=(pl.cdiv(m, MM_TM), nk),
        in_specs=[pl.BlockSpec((MM_TM, MM_TK), lambda i, kk: (i, ga0 + kk)),
                  pl.BlockSpec((MM_TM, MM_TK), lambda i, kk: (i, ga0 + nk + kk)),
                  pl.BlockSpec((1, MM_TK), lambda i, kk: (0, kk)),
                  pl.BlockSpec((1, MM_TK), lambda i, kk: (0, nk + kk)),
                  pl.BlockSpec((MM_TM, MM_TK), lambda i, kk: (i, kk)),
                  pl.BlockSpec((MM_TM, MM_TK), lambda i, kk: (i, kk)),
                  pl.BlockSpec((MM_TK, n), lambda i, kk: (kk, 0)),
                  pl.BlockSpec((MM_TM, n), lambda i, kk: (i, 0))],
        out_specs=pl.BlockSpec((MM_TM, n), lambda i, kk: (i, 0)),
        out_shape=jax.ShapeDtypeStruct((m, n), jnp.float32),
        compiler_params=pltpu.CompilerParams(dimension_semantics=("arbitrary", "arbitrary"),
                                             vmem_limit_bytes=MM_VMEM_BYTES),
        name="merge_out_proj",
    )(proj, proj, b_gate.reshape(1, 2 * k), b_gate.reshape(1, 2 * k), y_a, y_b, w.astype(jnp.bfloat16), h)


def _add_norm_kernel(a_ref, b_ref, g_ref, o_ref):
    o_ref[...] = _rms(a_ref[...] + b_ref[...], g_ref[...])


def _add_norm(a, b, g):
    m, k = a.shape
    return pl.pallas_call(
        _add_norm_kernel,
        grid=(pl.cdiv(m, MM_TM),),
        in_specs=[pl.BlockSpec((MM_TM, k), lambda i: (i, 0)),
                  pl.BlockSpec((MM_TM, k), lambda i: (i, 0)),
                  pl.BlockSpec((1, k), lambda i: (0, 0))],
        out_specs=pl.BlockSpec((MM_TM, k), lambda i: (i, 0)),
        out_shape=jax.ShapeDtypeStruct((m, k), jnp.float32),
        name="final_norm",
    )(a, b, g.reshape(1, k))


def _hyena_filter(L, w1, b1, w2, b2, w3, b3, freq):
    f32 = jnp.float32
    t = jnp.linspace(0.0, 1.0, L, dtype=f32)[:, None]
    w = (2.0 * math.pi / L) * jnp.arange(L, dtype=f32)[:, None]
    bands = jnp.linspace(1e-4, HY_BANDS - 1, HY_BANDS, dtype=f32)[None, :]
    z = jnp.concatenate([t, jnp.cos(bands * w), -jnp.sin(bands * w)], axis=-1)
    hdn = jnp.sin(freq[0] * (z @ w1 + b1))
    hdn = jnp.sin(freq[1] * (hdn @ w2 + b2))
    k = (hdn @ w3 + b3).reshape(L, 2, D_HY)
    deltas = jnp.abs(jnp.linspace(math.log(HY_TARGET) / HY_DECAY_HI,
                                  math.log(HY_TARGET) / HY_DECAY_LO, D_HY, dtype=f32))
    k = k * jnp.exp(-t * deltas)[:, None, :]
    return k[:, 0], k[:, 1]


HY_P = 256


def _longconv_kernel(kk_ref, u_ref, y_ref, acc_ref, *, nblk, nb):
    P = HY_P
    npad = nblk * P
    acc_ref[...] = jnp.zeros_like(acc_ref)
    for d in range(-(nblk - 1), nblk):
        start = npad + d * P - P
        w = kk_ref[0, :, start:start + 2 * P]
        m = pltpu.roll(jnp.broadcast_to(w, (P, 2 * P)), P, 1, stride=1, stride_axis=0)[:, :P]
        m = m.astype(jnp.bfloat16)
        if d >= 0:
            lhs = u_ref[0, 0:(nblk - d) * nb, :]
            acc_ref[d * nb:nblk * nb, :] += jnp.dot(lhs, m, preferred_element_type=jnp.float32)
        else:
            lhs = u_ref[0, (-d) * nb:nblk * nb, :]
            acc_ref[0:(nblk + d) * nb, :] += jnp.dot(lhs, m, preferred_element_type=jnp.float32)
    y_ref[0] = acc_ref[...]


def _long_conv(u, k_fwd, k_bwd):
    Bb, L, C = u.shape
    P = HY_P
    nblk = -(-L // P)
    npad = nblk * P
    kk = jnp.concatenate([jnp.zeros((npad - (L - 1), C), jnp.float32), k_bwd[:0:-1], k_fwd,
                          jnp.zeros((npad - L, C), jnp.float32)], axis=0)
    kk = kk.T.reshape(C, 1, 2 * npad)
    up = jnp.pad(u, ((0, 0), (0, npad - L), (0, 0))).astype(jnp.bfloat16)
    u2 = up.reshape(Bb, nblk, P, C).transpose(3, 1, 0, 2).reshape(C, nblk * Bb, P)
    y2 = pl.pallas_call(
        functools.partial(_longconv_kernel, nblk=nblk, nb=Bb),
        grid=(C,),
        in_specs=[pl.BlockSpec((1, 1, 2 * npad), lambda c: (c, 0, 0)),
                  pl.BlockSpec((1, nblk * Bb, P), lambda c: (c, 0, 0))],
        out_specs=pl.BlockSpec((1, nblk * Bb, P), lambda c: (c, 0, 0)),
        out_shape=jax.ShapeDtypeStruct((C, nblk * Bb, P), jnp.float32),
        scratch_shapes=[pltpu.VMEM((nblk * Bb, P), jnp.float32)],
        compiler_params=pltpu.CompilerParams(dimension_semantics=("arbitrary",)),
        name="hyena_longconv",
    )(kk, u2)
    y = y2.reshape(C, nblk, Bb, P).transpose(2, 1, 3, 0).reshape(Bb, npad, C)
    return y[:, :L]


def _hyena_branch(hy_in, conv_w, conv_b, w1, b1, w2, b2, w3, b3, freq, skip, w_proj):
    B, L, _ = hy_in.shape
    hp = jnp.pad(hy_in, ((0, 0), (1, 1), (0, 0)))
    u = hp[:, 0:L] * conv_w[0] + hp[:, 1:L + 1] * conv_w[1] + hp[:, 2:L + 2] * conv_w[2] + conv_b
    x0, x1, v = u[..., :D_HY], u[..., D_HY:2 * D_HY], u[..., 2 * D_HY:]
    k_fwd, k_bwd = _hyena_filter(L, w1, b1, w2, b2, w3, b3, freq)
    vx = v * x1
    y = (_long_conv(vx, k_fwd, k_bwd) + vx * skip) * x0
    return _matmul(y.reshape(B * L, D_HY), w_proj).reshape(B, L, D_MODEL)


RG_TL = 257


def _rglru_kernel(x_ref, cw_ref, cb_ref, gw_ref, gb_ref, lam_ref, o_ref, halo_ref, h_ref, a_ref, b_ref,
                  *, reverse, nb):
    c = pl.program_id(1)
    rows = x_ref.shape[0]
    tl = rows // nb
    hr = (RG_CONV - 1) * nb

    @pl.when(c == 0)
    def _():
        halo_ref[...] = jnp.zeros_like(halo_ref)
        h_ref[...] = jnp.zeros_like(h_ref)

    x = x_ref[...]
    if reverse:
        xe = jnp.concatenate([x, halo_ref[...]], axis=0)
        xc = cb_ref[...] + sum(xe[j * nb:j * nb + rows] * cw_ref[j:j + 1, :] for j in range(RG_CONV))
        halo_ref[...] = x[0:hr]
    else:
        xe = jnp.concatenate([halo_ref[...], x], axis=0)
        xc = cb_ref[...] + sum(xe[hr - j * nb:hr - j * nb + rows] * cw_ref[j:j + 1, :] for j in range(RG_CONV))
        halo_ref[...] = x[rows - hr:rows]
    xb = xc.astype(jnp.bfloat16)
    gr = jnp.dot(xb, gw_ref[0, 0], preferred_element_type=jnp.float32) + gb_ref[0:1, :]
    gi = jnp.dot(xb, gw_ref[1, 0], preferred_element_type=jnp.float32) + gb_ref[1:2, :]
    r = jax.nn.sigmoid(gr)
    i = jax.nn.sigmoid(gi)
    log_a = -RG_C * r * jax.nn.softplus(-lam_ref[...])
    a = jnp.exp(log_a)
    mult = jnp.sqrt(1.0 - jnp.exp(2.0 * log_a))
    row = lax.broadcasted_iota(jnp.int32, (rows, RG_HEAD_DIM), 0)
    first_rows = (row >= rows - nb) if reverse else (row < nb)
    mult = jnp.where(jnp.logical_and(first_rows, c == 0), 1.0, mult)
    a_ref[...] = a
    b_ref[...] = mult * i * xc

    def step(t, h):
        tt = (tl - 1 - t) if reverse else t
        r0 = pl.multiple_of(tt * nb, nb)
        h = a_ref[pl.ds(r0, nb), :] * h + b_ref[pl.ds(r0, nb), :]
        o_ref[pl.ds(r0, nb), :] = h
        return h

    h_ref[...] = lax.fori_loop(0, tl, step, h_ref[...], unroll=8)


def _rglru_direction(xt, nb, conv_w, conv_b, gate_w, gate_b, lam, reverse):
    T = xt.shape[0]
    L = T // nb
    assert L % RG_TL == 0 and nb % 8 == 0
    nch = L // RG_TL
    rows = RG_TL * nb
    cmap = (lambda c: nch - 1 - c) if reverse else (lambda c: c)
    return pl.pallas_call(
        functools.partial(_rglru_kernel, reverse=reverse, nb=nb),
        grid=(RG_HEADS, nch),
        in_specs=[
            pl.BlockSpec((rows, RG_HEAD_DIM), lambda h, c: (cmap(c), h)),
            pl.BlockSpec((RG_CONV, RG_HEAD_DIM), lambda h, c: (0, h)),
            pl.BlockSpec((1, RG_HEAD_DIM), lambda h, c: (0, h)),
            pl.BlockSpec((2, 1, RG_HEAD_DIM, RG_HEAD_DIM), lambda h, c: (0, h, 0, 0)),
            pl.BlockSpec((2, RG_HEAD_DIM), lambda h, c: (0, h)),
            pl.BlockSpec((1, RG_HEAD_DIM), lambda h, c: (0, h)),
        ],
        out_specs=pl.BlockSpec((rows, RG_HEAD_DIM), lambda h, c: (cmap(c), h)),
        out_shape=jax.ShapeDtypeStruct((T, D_RG), jnp.float32),
        scratch_shapes=[pltpu.VMEM(((RG_CONV - 1) * nb, RG_HEAD_DIM), jnp.float32),
                        pltpu.VMEM((nb, RG_HEAD_DIM), jnp.float32),
                        pltpu.VMEM((rows, RG_HEAD_DIM), jnp.float32),
                        pltpu.VMEM((rows, RG_HEAD_DIM), jnp.float32)],
        compiler_params=pltpu.CompilerParams(dimension_semantics=("arbitrary", "arbitrary")),
        name="rglru_bwd" if reverse else "rglru_fwd",
    )(xt, conv_w, conv_b.reshape(1, D_RG), gate_w.astype(jnp.bfloat16), gate_b, lam.reshape(1, D_RG))


def _rglru_states(rg_x, conv_w, conv_b, gate_w, gate_b, lam):
    B, L, _ = rg_x.shape
    xt = rg_x.transpose(1, 0, 2).reshape(L * B, D_RG)
    h_f = _rglru_direction(xt, B, conv_w[0], conv_b[0], gate_w[0], gate_b[0], lam[0], False)
    h_b = _rglru_direction(xt, B, conv_w[1], conv_b[1], gate_w[1], gate_b[1], lam[1], True)
    return (h_f + h_b).reshape(L, B, D_RG).transpose(1, 0, 2).reshape(B * L, D_RG)


PEER_HK = PEER_HEADS * PEER_TOPK
PEER_LANES = 128
PEER_ROWS = D_MODEL // PEER_LANES
PEER_TT = 128
PEER_G = 8
PEER_NG = PEER_TT // PEER_G
PEER_NSLOT = 4
PEER_AHEAD = 2
PEER_TX = PEER_TT + PEER_AHEAD * PEER_G


def _split_bf16(x):
    hi = x.astype(jnp.bfloat16)
    lo = (x - hi.astype(jnp.float32)).astype(jnp.bfloat16)
    return hi, lo


def _peer_kernel(idx_ref, g_ref, x_ref, tab_ref, sel_ref, exp_ref, mask_ref, out_ref, *scratch):
    bufs = scratch[:PEER_NSLOT]
    sem = scratch[PEER_NSLOT]
    step = pl.program_id(0)
    last = pl.num_programs(0) - 1
    G, HK, R = PEER_G, PEER_HK, PEER_ROWS

    def issue(tok0, slot):
        for j in range(G):
            for k in range(HK):
                e = idx_ref[0, tok0 + j, k]
                pltpu.make_async_copy(tab_ref.at[e], bufs[slot].at[j * HK + k], sem.at[slot]).start(priority=k % 2)

    def wait(slot):
        pltpu.make_async_copy(tab_ref.at[pl.ds(0, G * HK)], bufs[slot], sem.at[slot]).wait()

    mask = mask_ref[...]

    def compute(tok0, slot):
        buf = bufs[slot]
        xs = x_ref[pl.ds(tok0, G)]
        vbs = []
        rms = []
        for j in range(G):
            w = buf[j * HK:(j + 1) * HK].reshape(HK * R, PEER_LANES)
            ub = pltpu.bitcast(w << 16, jnp.float32).astype(jnp.bfloat16)
            vbs.append(pltpu.bitcast(w & jnp.uint32(0xFFFF0000), jnp.float32).astype(jnp.bfloat16))
            xh, xl = _split_bf16(xs[j])
            x16 = jnp.concatenate([xh, xl], axis=0)
            rt = lax.dot_general(x16, ub, (((1,), (1,)), ((), ())), preferred_element_type=jnp.float32)
            rms.append((rt[0:R] + rt[R:]) * mask)
        rm = jnp.concatenate(rms, axis=0)
        rh, rl = _split_bf16(rm)
        z = jnp.dot(jnp.concatenate([rh, rl], axis=0), sel_ref[...], preferred_element_type=jnp.float32)
        z = z[0:G * R] + z[G * R:]
        act = jnp.sum(z.reshape(G, R, HK), axis=1)
        a = jax.nn.gelu(act) * g_ref[pl.ds(tok0, G), :]
        ah, al = _split_bf16(a)
        arep = jnp.dot(jnp.concatenate([ah, al], axis=0), exp_ref[...], preferred_element_type=jnp.float32)
        arep = arep[0:G] + arep[G:]
        for j in range(G):
            am = jnp.broadcast_to(arep[j:j + 1], (R, HK * R)) * mask
            amh, aml = _split_bf16(am)
            o = jnp.dot(jnp.concatenate([amh, aml], axis=0), vbs[j], preferred_element_type=jnp.float32)
            out_ref[tok0 + j] = o[0:R] + o[R:]

    @pl.when(step == 0)
    def _():
        for a in range(PEER_AHEAD):
            issue(a * G, a)

    def body(r, c):
        for s in range(PEER_NSLOT):
            tok0 = pl.multiple_of((r * PEER_NSLOT + s) * G, G)
            wait(s)
            issue(tok0 + PEER_AHEAD * G, (s + PEER_AHEAD) % PEER_NSLOT)
            compute(tok0, s)
        return c

    lax.fori_loop(0, PEER_NG // PEER_NSLOT, body, 0)

    @pl.when(step == last)
    def _():
        for a in range(PEER_AHEAD):
            wait(a)


def _peer_experts(idx, g, x3, table):
    T = idx.shape[0]
    ns = T // PEER_TT
    HK, R = PEER_HK, PEER_ROWS
    r = np.arange(HK * R)
    sel = jnp.asarray(r[:, None] // R == np.arange(HK)[None, :], jnp.bfloat16)
    mask = jnp.asarray(r[None, :] % R == np.arange(R)[:, None], jnp.float32)
    idx3 = idx.reshape(ns, PEER_TT, HK)
    ahead = PEER_AHEAD * PEER_G
    nxt = jnp.concatenate([idx3[1:, :ahead], idx3[-1:, :ahead]], axis=0)
    idx_ext = jnp.concatenate([idx3, nxt], axis=1)
    return pl.pallas_call(
        _peer_kernel,
        grid=(ns,),
        in_specs=[
            pl.BlockSpec((1, PEER_TX, HK), lambda i: (i, 0, 0), memory_space=pltpu.SMEM),
            pl.BlockSpec((PEER_TT, HK), lambda i: (i, 0)),
            pl.BlockSpec((PEER_TT, R, PEER_LANES), lambda i: (i, 0, 0)),
            pl.BlockSpec(memory_space=pl.ANY),
            pl.BlockSpec((HK * R, HK), lambda i: (0, 0)),
            pl.BlockSpec((HK, HK * R), lambda i: (0, 0)),
            pl.BlockSpec((R, HK * R), lambda i: (0, 0)),
        ],
        out_specs=pl.BlockSpec((PEER_TT, R, PEER_LANES), lambda i: (i, 0, 0)),
        out_shape=jax.ShapeDtypeStruct((T, R, PEER_LANES), jnp.float32),
        scratch_shapes=[pltpu.VMEM((PEER_G * HK, R, PEER_LANES), jnp.uint32) for _ in range(PEER_NSLOT)]
        + [pltpu.SemaphoreType.DMA((PEER_NSLOT,))],
        compiler_params=pltpu.CompilerParams(dimension_semantics=("arbitrary",)),
        name="peer_experts",
    )(idx_ext, g, x3, table, sel, sel.T, mask)


def _pack_expert_tables(u, v):
    ub = lax.bitcast_convert_type(u.astype(jnp.bfloat16), jnp.uint16).astype(jnp.uint32)
    vb = lax.bitcast_convert_type(v.astype(jnp.bfloat16), jnp.uint16).astype(jnp.uint32)
    return ((vb << 16) | ub).reshape(u.shape[0], PEER_ROWS, PEER_LANES)


ROUTE_TT = 128
ROUTE_GROUP = 8


def _cand_layout():
    K = PEER_TOPK
    ii, jj, ok = [], [], []

    def add(i_list, j_list, ok_list):
        ii.extend(i_list)
        jj.extend(j_list)
        ok.extend(ok_list)

    add([0] * 8, list(range(8)), [True] * 8)
    add([0] * 8, list(range(8, 16)), [True] * 8)
    for i in range(1, 8):
        jmax = K // (i + 1) - 1
        add([i] * 8, list(range(8)), [j <= jmax for j in range(8)])
    add(list(range(8, 16)), [0] * 8, [True] * 8)
    return np.array(ii), np.array(jj), np.array(ok)


_CAND_I, _CAND_J, _CAND_OK = _cand_layout()
ROUTE_NC = len(_CAND_I)


def _top16(s):
    K, TT = PEER_TOPK, ROUTE_TT
    kidx = lax.broadcasted_iota(jnp.int32, (N_KEYS, TT), 0)
    sub16 = lax.broadcasted_iota(jnp.int32, (K, TT), 0)
    ninf = jnp.float32(-jnp.inf)
    ts = jnp.zeros((K, TT), jnp.float32)
    ti = jnp.zeros((K, TT), jnp.int32)
    for r in range(K):
        m = jnp.max(s, axis=0, keepdims=True)
        imin = jnp.min(jnp.where(s == m, kidx, N_KEYS), axis=0, keepdims=True)
        s = jnp.where(kidx == imin, ninf, s)
        ts = jnp.where(sub16 == r, m, ts)
        ti = jnp.where(sub16 == r, imin, ti)
    return ts, ti


def _head_route(s0, s1, flat, neg):
    K, TT = PEER_TOPK, ROUTE_TT
    sub16 = lax.broadcasted_iota(jnp.int32, (K, TT), 0)
    ninf = jnp.float32(-jnp.inf)
    ts0, ti0 = _top16(s0)
    ts1, ti1 = _top16(s1)
    e0 = ti0 * N_KEYS

    def row(a, i):
        return jnp.broadcast_to(a[i:i + 1, :], (ROUTE_GROUP, TT))

    cv = [row(ts0, 0) + ts1[0:8], row(ts0, 0) + ts1[8:16]]
    ev = [row(e0, 0) + ti1[0:8], row(e0, 0) + ti1[8:16]]
    for i in range(1, 8):
        cv.append(row(ts0, i) + ts1[0:8])
        ev.append(row(e0, i) + ti1[0:8])
    cv.append(ts0[8:16] + row(ts1, 0))
    ev.append(e0[8:16] + row(ti1, 0))
    cand = jnp.concatenate(cv, axis=0) + neg
    eid = jnp.concatenate(ev, axis=0)
    cs = jnp.zeros((K, TT), jnp.float32)
    ei = jnp.zeros((K, TT), jnp.int32)
    m0 = None
    for r in range(K):
        m = jnp.max(cand, axis=0, keepdims=True)
        if r == 0:
            m0 = m
        fmin = jnp.min(jnp.where(cand == m, flat, 4 * K * K), axis=0, keepdims=True)
        hit = flat == fmin
        e = jnp.max(jnp.where(hit, eid, -1), axis=0, keepdims=True)
        cand = jnp.where(hit, ninf, cand)
        cs = jnp.where(sub16 == r, m, cs)
        ei = jnp.where(sub16 == r, e, ei)
    p = jnp.exp(cs - m0)
    g = p / jnp.sum(p, axis=0, keepdims=True)
    return ei, g


def _route_kernel(h_ref, gn_ref, ws_ref, flat_ref, neg_ref, xn_ref, idx_ref, g_ref, s_ref, it_ref, gt_ref):
    x = h_ref[...]
    xn = x * lax.rsqrt(jnp.mean(x * x, axis=-1, keepdims=True) + EPS) * gn_ref[...]
    xn_ref[...] = xn
    s_ref[...] = lax.dot_general(ws_ref[...], xn.astype(jnp.bfloat16), (((1,), (1,)), ((), ())),
                                 preferred_element_type=jnp.float32)
    flat = flat_ref[...]
    neg = neg_ref[...]

    def head(h, c):
        base = pl.multiple_of(h * 2 * N_KEYS, 2 * N_KEYS)
        ei, g = _head_route(s_ref[pl.ds(base, N_KEYS), :], s_ref[pl.ds(base + N_KEYS, N_KEYS), :], flat, neg)
        ob = pl.multiple_of(h * PEER_TOPK, PEER_TOPK)
        it_ref[pl.ds(ob, PEER_TOPK), :] = ei
        gt_ref[pl.ds(ob, PEER_TOPK), :] = g
        return c

    lax.fori_loop(0, PEER_HEADS, head, 0)
    idx_ref[...] = it_ref[...].T
    g_ref[...] = gt_ref[...].T


def _peer_route(h2, g_ffn, ws_t):
    T = h2.shape[0]
    TT, K, NC, HK = ROUTE_TT, PEER_TOPK, ROUTE_NC, PEER_HK
    nrow = PEER_HEADS * 2 * N_KEYS
    flat = jnp.asarray(np.broadcast_to((_CAND_I * K + _CAND_J)[:, None], (NC, TT)), jnp.int32)
    neg = jnp.asarray(np.broadcast_to(np.where(_CAND_OK, 0.0, -np.inf)[:, None], (NC, TT)), jnp.float32)
    return pl.pallas_call(
        _route_kernel,
        grid=(T // TT,),
        in_specs=[
            pl.BlockSpec((TT, D_MODEL), lambda i: (i, 0)),
            pl.BlockSpec((1, D_MODEL), lambda i: (0, 0)),
            pl.BlockSpec((nrow, D_MODEL), lambda i: (0, 0)),
            pl.BlockSpec((NC, TT), lambda i: (0, 0)),
            pl.BlockSpec((NC, TT), lambda i: (0, 0)),
        ],
        out_specs=[
            pl.BlockSpec((TT, D_MODEL), lambda i: (i, 0)),
            pl.BlockSpec((TT, HK), lambda i: (i, 0)),
            pl.BlockSpec((TT, HK), lambda i: (i, 0)),
        ],
        out_shape=[
            jax.ShapeDtypeStruct((T, D_MODEL), jnp.float32),
            jax.ShapeDtypeStruct((T, HK), jnp.int32),
            jax.ShapeDtypeStruct((T, HK), jnp.float32),
        ],
        scratch_shapes=[pltpu.VMEM((nrow, TT), jnp.float32),
                        pltpu.VMEM((HK, TT), jnp.int32),
                        pltpu.VMEM((HK, TT), jnp.float32)],
        compiler_params=pltpu.CompilerParams(dimension_semantics=("arbitrary",)),
        name="peer_route",
    )(h2, g_ffn.reshape(1, D_MODEL), ws_t, flat, neg)


def _score_weights(wq, keys):
    n = PEER_HEADS * 2
    eye = jnp.eye(n, dtype=keys.dtype)
    kd = keys.reshape(n, N_KEYS, D_HALF)
    block_diag = (eye[:, None, :, None] * kd[:, :, None, :]).reshape(n * N_KEYS, n * D_HALF)
    return _matmul(block_diag, wq.T, tm=512, tn=512).astype(jnp.bfloat16)


def _peer(h2, g_ffn, wq, keys, u_tab, v_tab):
    T = h2.shape[0]
    xn, experts, g = _peer_route(h2, g_ffn, _score_weights(wq, keys))
    out = _peer_experts(experts, g, xn.reshape(T, PEER_ROWS, PEER_LANES), _pack_expert_tables(u_tab, v_tab))
    return out.reshape(T, D_MODEL)


def kernel(x, meta, g_final, g_mix, w_in, b_gate, hy_conv_w, hy_conv_b, hy_w1, hy_b1, hy_w2, hy_b2,
           hy_w3, hy_b3, hy_freq, hy_skip, hy_out, rg_conv_w, rg_conv_b, rg_gate_w, rg_gate_b,
           rg_lambda, rg_out, w_out, g_ffn, peer_wq, peer_keys, peer_u, peer_v):
    B = x.shape[0]
    h = jnp.concatenate([jnp.broadcast_to(meta[None], (B, N_META, D_MODEL)), x], axis=1)
    L = h.shape[1]
    s_hy = 3 * D_HY
    s_rx = s_hy + D_RG
    s_rg = s_rx + D_RG
    l = 0
    h0 = h.reshape(B * L, D_MODEL)
    proj2 = _norm_matmul(h0, g_mix[l], w_in[l])
    proj = proj2.reshape(B, L, -1)
    y_hy = _hyena_branch(proj[..., :s_hy], hy_conv_w[l], hy_conv_b[l], hy_w1[l], hy_b1[l], hy_w2[l],
                         hy_b2[l], hy_w3[l], hy_b3[l], hy_freq[l], hy_skip[l], hy_out[l])
    rg_h = _rglru_states(proj[..., s_hy:s_rx], rg_conv_w[l], rg_conv_b[l], rg_gate_w[l], rg_gate_b[l],
                         rg_lambda[l])
    y_rg = _gelu_gate_matmul(rg_h, proj2, s_rx, rg_out[l])
    h2 = _merge_matmul(proj2, s_rg, b_gate[l], y_hy.reshape(B * L, D_MODEL), y_rg, w_out[l], h0)
    p = _peer(h2, g_ffn[l], peer_wq[l], peer_keys[l], peer_u[l], peer_v[l])
    out = _add_norm(h2, p, g_final).reshape(B, L, D_MODEL)
    return out[:, N_META:]
```

```python
import functools
import math

import jax
import jax.numpy as jnp
import numpy as np
from jax import lax
from jax.experimental import pallas as pl
from jax.experimental.pallas import tpu as pltpu

D_MODEL = 1024
N_META = 16
EPS = 1e-6
D_HY = 512
HY_EMB = 33
HY_BANDS = (HY_EMB - 1) // 2
HY_TARGET = 1e-2
HY_DECAY_HI = 0.3
HY_DECAY_LO = 1.5
D_RG = 1024
RG_HEADS = 8
RG_HEAD_DIM = D_RG // RG_HEADS
RG_CONV = 4
RG_C = 8.0
PEER_HEADS = 8
N_KEYS = 128
D_KEY = 256
D_HALF = D_KEY // 2
PEER_TOPK = 16
PEER_CHUNK = 16


def _mm_kernel(a_ref, b_ref, o_ref):
    o_ref[...] = jnp.dot(a_ref[...].astype(jnp.bfloat16), b_ref[...],
                         preferred_element_type=jnp.float32)


def _matmul(a, b, tm=512, tn=512):
    m, k = a.shape
    _, n = b.shape
    tn = min(tn, n)
    return pl.pallas_call(
        _mm_kernel,
        grid=(pl.cdiv(m, tm), n // tn),
        in_specs=[pl.BlockSpec((tm, k), lambda i, j: (i, 0)),
                  pl.BlockSpec((k, tn), lambda i, j: (0, j))],
        out_specs=pl.BlockSpec((tm, tn), lambda i, j: (i, j)),
        out_shape=jax.ShapeDtypeStruct((m, n), jnp.float32),
    )(a, b.astype(jnp.bfloat16))


MM_TM = 512
MM_TK = 512
MM_VMEM_BYTES = 48 * 1024 * 1024


def _rms(x, g):
    return x * lax.rsqrt(jnp.mean(x * x, axis=-1, keepdims=True) + EPS) * g


def _norm_mm_kernel(a_ref, g_ref, b_ref, o_ref):
    nb = _rms(a_ref[...], g_ref[...]).astype(jnp.bfloat16)
    o_ref[...] = jnp.dot(nb, b_ref[...], preferred_element_type=jnp.float32)


def _norm_matmul(a, g, b, tm=256):
    m, k = a.shape
    n = b.shape[1]
    return pl.pallas_call(
        _norm_mm_kernel,
        grid=(pl.cdiv(m, tm),),
        in_specs=[pl.BlockSpec((tm, k), lambda i: (i, 0)),
                  pl.BlockSpec((1, k), lambda i: (0, 0)),
                  pl.BlockSpec((k, n), lambda i: (0, 0))],
        out_specs=pl.BlockSpec((tm, n), lambda i: (i, 0)),
        out_shape=jax.ShapeDtypeStruct((m, n), jnp.float32),
        compiler_params=pltpu.CompilerParams(dimension_semantics=("arbitrary",),
                                             vmem_limit_bytes=MM_VMEM_BYTES),
        name="norm_in_proj",
    )(a, g.reshape(1, k), b.astype(jnp.bfloat16))


def _gelu_mm_kernel(h_ref, gate_ref, w_ref, o_ref):
    y = h_ref[...] * jax.nn.gelu(gate_ref[...])
    d = jnp.dot(y.astype(jnp.bfloat16), w_ref[...], preferred_element_type=jnp.float32)

    @pl.when(pl.program_id(1) == 0)
    def _():
        o_ref[...] = d

    @pl.when(pl.program_id(1) != 0)
    def _():
        o_ref[...] += d


def _gelu_gate_matmul(hsum, proj, gate_col, w):
    m, k = hsum.shape
    n = w.shape[1]
    assert gate_col % MM_TK == 0 and k % MM_TK == 0
    gb = gate_col // MM_TK
    return pl.pallas_call(
        _gelu_mm_kernel,
        grid=(pl.cdiv(m, MM_TM), k // MM_TK),
        in_specs=[pl.BlockSpec((MM_TM, MM_TK), lambda i, kk: (i, kk)),
                  pl.BlockSpec((MM_TM, MM_TK), lambda i, kk: (i, gb + kk)),
                  pl.BlockSpec((MM_TK, n), lambda i, kk: (kk, 0))],
        out_specs=pl.BlockSpec((MM_TM, n), lambda i, kk: (i, 0)),
        out_shape=jax.ShapeDtypeStruct((m, n), jnp.float32),
        compiler_params=pltpu.CompilerParams(dimension_semantics=("arbitrary", "arbitrary"),
                                             vmem_limit_bytes=MM_VMEM_BYTES),
        name="rg_out_proj",
    )(hsum, proj, w.astype(jnp.bfloat16))


def _merge_mm_kernel(ga_ref, gb_ref, ba_ref, bb_ref, ya_ref, yb_ref, w_ref, h_ref, o_ref):
    m = (jax.nn.sigmoid(ga_ref[...] + ba_ref[...]) * ya_ref[...]
         + jax.nn.sigmoid(gb_ref[...] + bb_ref[...]) * yb_ref[...])
    d = jnp.dot(m.astype(jnp.bfloat16), w_ref[...], preferred_element_type=jnp.float32)

    @pl.when(pl.program_id(1) == 0)
    def _():
        o_ref[...] = h_ref[...] + d

    @pl.when(pl.program_id(1) != 0)
    def _():
        o_ref[...] += d


def _merge_matmul(proj, gate_col, b_gate, y_a, y_b, w, h):
    m, k = y_a.shape
    n = w.shape[1]
    assert gate_col % MM_TK == 0 and k % MM_TK == 0
    nk = k // MM_TK
    ga0 = gate_col // MM_TK
    return pl.pallas_call(
        _merge_mm_kernel,
        grid=(pl.cdiv(m, MM_TM), nk),
        in_specs=[pl.BlockSpec((MM_TM, MM_TK), lambda i, kk: (i, ga0 + kk)),
                  pl.BlockSpec((MM_TM, MM_TK), lambda i, kk: (i, ga0 + nk + kk)),
                  pl.BlockSpec((1, MM_TK), lambda i, kk: (0, kk)),
                  pl.BlockSpec((1, MM_TK), lambda i, kk: (0, nk + kk)),
                  pl.BlockSpec((MM_TM, MM_TK), lambda i, kk: (i, kk)),
                  pl.BlockSpec((MM_TM, MM_TK), lambda i, kk: (i, kk)),
                  pl.BlockSpec((MM_TK, n), lambda i, kk: (kk, 0)),
                  pl.BlockSpec((MM_TM, n), lambda i, kk: (i, 0))],
        out_specs=pl.BlockSpec((MM_TM, n), lambda i, kk: (i, 0)),
        out_shape=jax.ShapeDtypeStruct((m, n), jnp.float32),
        compiler_params=pltpu.CompilerParams(dimension_semantics=("arbitrary", "arbitrary"),
                                             vmem_limit_bytes=MM_VMEM_BYTES),
        name="merge_out_proj",
    )(proj, proj, b_gate.reshape(1, 2 * k), b_gate.reshape(1, 2 * k), y_a, y_b, w.astype(jnp.bfloat16), h)


def _add_norm_kernel(a_ref, b_ref, g_ref, o_ref):
    o_ref[...] = _rms(a_ref[...] + b_ref[...], g_ref[...])


def _add_norm(a, b, g):
    m, k = a.shape
    return pl.pallas_call(
        _add_norm_kernel,
        grid=(pl.cdiv(m, MM_TM),),
        in_specs=[pl.BlockSpec((MM_TM, k), lambda i: (i, 0)),
                  pl.BlockSpec((MM_TM, k), lambda i: (i, 0)),
                  pl.BlockSpec((1, k), lambda i: (0, 0))],
        out_specs=pl.BlockSpec((MM_TM, k), lambda i: (i, 0)),
        out_shape=jax.ShapeDtypeStruct((m, k), jnp.float32),
        name="final_norm",
    )(a, b, g.reshape(1, k))


def _hyena_filter(L, w1, b1, w2, b2, w3, b3, freq):
    f32 = jnp.float32
    t = jnp.linspace(0.0, 1.0, L, dtype=f32)[:, None]
    w = (2.0 * math.pi / L) * jnp.arange(L, dtype=f32)[:, None]
    bands = jnp.linspace(1e-4, HY_BANDS - 1, HY_BANDS, dtype=f32)[None, :]
    z = jnp.concatenate([t, jnp.cos(bands * w), -jnp.sin(bands * w)], axis=-1)
    hdn = jnp.sin(freq[0] * (z @ w1 + b1))
    hdn = jnp.sin(freq[1] * (hdn @ w2 + b2))
    k = (hdn @ w3 + b3).reshape(L, 2, D_HY)
    deltas = jnp.abs(jnp.linspace(math.log(HY_TARGET) / HY_DECAY_HI,
                                  math.log(HY_TARGET) / HY_DECAY_LO, D_HY, dtype=f32))
    k = k * jnp.exp(-t * deltas)[:, None, :]
    return k[:, 0], k[:, 1]


HY_P = 256


def _longconv_kernel(kk_ref, u_ref, y_ref, acc_ref, *, nblk, nb):
    P = HY_P
    npad = nblk * P
    acc_ref[...] = jnp.zeros_like(acc_ref)
    for d in range(-(nblk - 1), nblk):
        start = npad + d * P - P
        w = kk_ref[0, :, start:start + 2 * P]
        m = pltpu.roll(jnp.broadcast_to(w, (P, 2 * P)), P, 1, stride=1, stride_axis=0)[:, :P]
        m = m.astype(jnp.bfloat16)
        if d >= 0:
            lhs = u_ref[0, 0:(nblk - d) * nb, :]
            acc_ref[d * nb:nblk * nb, :] += jnp.dot(lhs, m, preferred_element_type=jnp.float32)
        else:
            lhs = u_ref[0, (-d) * nb:nblk * nb, :]
            acc_ref[0:(nblk + d) * nb, :] += jnp.dot(lhs, m, preferred_element_type=jnp.float32)
    y_ref[0] = acc_ref[...]


def _long_conv(u, k_fwd, k_bwd):
    Bb, L, C = u.shape
    P = HY_P
    nblk = -(-L // P)
    npad = nblk * P
    kk = jnp.concatenate([jnp.zeros((npad - (L - 1), C), jnp.float32), k_bwd[:0:-1], k_fwd,
                          jnp.zeros((npad - L, C), jnp.float32)], axis=0)
    kk = kk.T.reshape(C, 1, 2 * npad)
    up = jnp.pad(u, ((0, 0), (0, npad - L), (0, 0))).astype(jnp.bfloat16)
    u2 = up.reshape(Bb, nblk, P, C).transpose(3, 1, 0, 2).reshape(C, nblk * Bb, P)
    y2 = pl.pallas_call(
        functools.partial(_longconv_kernel, nblk=nblk, nb=Bb),
        grid=(C,),
        in_specs=[pl.BlockSpec((1, 1, 2 * npad), lambda c: (c, 0, 0)),
                  pl.BlockSpec((1, nblk * Bb, P), lambda c: (c, 0, 0))],
        out_specs=pl.BlockSpec((1, nblk * Bb, P), lambda c: (c, 0, 0)),
        out_shape=jax.ShapeDtypeStruct((C, nblk * Bb, P), jnp.float32),
        scratch_shapes=[pltpu.VMEM((nblk * Bb, P), jnp.float32)],
        compiler_params=pltpu.CompilerParams(dimension_semantics=("arbitrary",)),
        name="hyena_longconv",
    )(kk, u2)
    y = y2.reshape(C, nblk, Bb, P).transpose(2, 1, 3, 0).reshape(Bb, npad, C)
    return y[:, :L]


def _hyena_branch(hy_in, conv_w, conv_b, w1, b1, w2, b2, w3, b3, freq, skip, w_proj):
    B, L, _ = hy_in.shape
    hp = jnp.pad(hy_in, ((0, 0), (1, 1), (0, 0)))
    u = hp[:, 0:L] * conv_w[0] + hp[:, 1:L + 1] * conv_w[1] + hp[:, 2:L + 2] * conv_w[2] + conv_b
    x0, x1, v = u[..., :D_HY], u[..., D_HY:2 * D_HY], u[..., 2 * D_HY:]
    k_fwd, k_bwd = _hyena_filter(L, w1, b1, w2, b2, w3, b3, freq)
    vx = v * x1
    y = (_long_conv(vx, k_fwd, k_bwd) + vx * skip) * x0
    return _matmul(y.reshape(B * L, D_HY), w_proj).reshape(B, L, D_MODEL)


RG_TL = 257


def _rglru_kernel(x_ref, cw_ref, cb_ref, gw_ref, gb_ref, lam_ref, o_ref, halo_ref, h_ref, a_ref, b_ref,
                  *, reverse, nb):
    c = pl.program_id(1)
    rows = x_ref.shape[0]
    tl = rows // nb
    hr = (RG_CONV - 1) * nb

    @pl.when(c == 0)
    def _():
        halo_ref[...] = jnp.zeros_like(halo_ref)
        h_ref[...] = jnp.zeros_like(h_ref)

    x = x_ref[...]
    if reverse:
        xe = jnp.concatenate([x, halo_ref[...]], axis=0)
        xc = cb_ref[...] + sum(xe[j * nb:j * nb + rows] * cw_ref[j:j + 1, :] for j in range(RG_CONV))
        halo_ref[...] = x[0:hr]
    else:
        xe = jnp.concatenate([halo_ref[...], x], axis=0)
        xc = cb_ref[...] + sum(xe[hr - j * nb:hr - j * nb + rows] * cw_ref[j:j + 1, :] for j in range(RG_CONV))
        halo_ref[...] = x[rows - hr:rows]
    xb = xc.astype(jnp.bfloat16)
    gr = jnp.dot(xb, gw_ref[0, 0], preferred_element_type=jnp.float32) + gb_ref[0:1, :]
    gi = jnp.dot(xb, gw_ref[1, 0], preferred_element_type=jnp.float32) + gb_ref[1:2, :]
    r = jax.nn.sigmoid(gr)
    i = jax.nn.sigmoid(gi)
    log_a = -RG_C * r * jax.nn.softplus(-lam_ref[...])
    a = jnp.exp(log_a)
    mult = jnp.sqrt(1.0 - jnp.exp(2.0 * log_a))
    row = lax.broadcasted_iota(jnp.int32, (rows, RG_HEAD_DIM), 0)
    first_rows = (row >= rows - nb) if reverse else (row < nb)
    mult = jnp.where(jnp.logical_and(first_rows, c == 0), 1.0, mult)
    a_ref[...] = a
    b_ref[...] = mult * i * xc

    def step(t, h):
        tt = (tl - 1 - t) if reverse else t
        r0 = pl.multiple_of(tt * nb, nb)
        h = a_ref[pl.ds(r0, nb), :] * h + b_ref[pl.ds(r0, nb), :]
        o_ref[pl.ds(r0, nb), :] = h
        return h

    h_ref[...] = lax.fori_loop(0, tl, step, h_ref[...], unroll=8)


def _rglru_direction(xt, nb, conv_w, conv_b, gate_w, gate_b, lam, reverse):
    T = xt.shape[0]
    L = T // nb
    assert L % RG_TL == 0 and nb % 8 == 0
    nch = L // RG_TL
    rows = RG_TL * nb
    cmap = (lambda c: nch - 1 - c) if reverse else (lambda c: c)
    return pl.pallas_call(
        functools.partial(_rglru_kernel, reverse=reverse, nb=nb),
        grid=(RG_HEADS, nch),
        in_specs=[
            pl.BlockSpec((rows, RG_HEAD_DIM), lambda h, c: (cmap(c), h)),
            pl.BlockSpec((RG_CONV, RG_HEAD_DIM), lambda h, c: (0, h)),
            pl.BlockSpec((1, RG_HEAD_DIM), lambda h, c: (0, h)),
            pl.BlockSpec((2, 1, RG_HEAD_DIM, RG_HEAD_DIM), lambda h, c: (0, h, 0, 0)),
            pl.BlockSpec((2, RG_HEAD_DIM), lambda h, c: (0, h)),
            pl.BlockSpec((1, RG_HEAD_DIM), lambda h, c: (0, h)),
        ],
        out_specs=pl.BlockSpec((rows, RG_HEAD_DIM), lambda h, c: (cmap(c), h)),
        out_shape=jax.ShapeDtypeStruct((T, D_RG), jnp.float32),
        scratch_shapes=[pltpu.VMEM(((RG_CONV - 1) * nb, RG_HEAD_DIM), jnp.float32),
                        pltpu.VMEM((nb, RG_HEAD_DIM), jnp.float32),
                        pltpu.VMEM((rows, RG_HEAD_DIM), jnp.float32),
                        pltpu.VMEM((rows, RG_HEAD_DIM), jnp.float32)],
        compiler_params=pltpu.CompilerParams(dimension_semantics=("arbitrary", "arbitrary")),
        name="rglru_bwd" if reverse else "rglru_fwd",
    )(xt, conv_w, conv_b.reshape(1, D_RG), gate_w.astype(jnp.bfloat16), gate_b, lam.reshape(1, D_RG))


def _rglru_states(rg_x, conv_w, conv_b, gate_w, gate_b, lam):
    B, L, _ = rg_x.shape
    xt = rg_x.transpose(1, 0, 2).reshape(L * B, D_RG)
    h_f = _rglru_direction(xt, B, conv_w[0], conv_b[0], gate_w[0], gate_b[0], lam[0], False)
    h_b = _rglru_direction(xt, B, conv_w[1], conv_b[1], gate_w[1], gate_b[1], lam[1], True)
    return (h_f + h_b).reshape(L, B, D_RG).transpose(1, 0, 2).reshape(B * L, D_RG)


PEER_HK = PEER_HEADS * PEER_TOPK
PEER_LANES = 128
PEER_ROWS = D_MODEL // PEER_LANES
PEER_TT = 128
PEER_G = 8
PEER_NG = PEER_TT // PEER_G
PEER_NSLOT = 4
PEER_AHEAD = 2
PEER_TX = PEER_TT + PEER_AHEAD * PEER_G


def _split_bf16(x):
    hi = x.astype(jnp.bfloat16)
    lo = (x - hi.astype(jnp.float32)).astype(jnp.bfloat16)
    return hi, lo


def _peer_kernel(idx_ref, g_ref, x_ref, tab_ref, sel_ref, exp_ref, mask_ref, out_ref, *scratch):
    bufs = scratch[:PEER_NSLOT]
    sem = scratch[PEER_NSLOT]
    step = pl.program_id(0)
    last = pl.num_programs(0) - 1
    G, HK, R = PEER_G, PEER_HK, PEER_ROWS

    def issue(tok0, slot):
        for j in range(G):
            for k in range(HK):
                e = idx_ref[0, tok0 + j, k]
                pltpu.make_async_copy(tab_ref.at[e], bufs[slot].at[j * HK + k], sem.at[slot]).start(priority=k % 2)

    def wait(slot):
        pltpu.make_async_copy(tab_ref.at[pl.ds(0, G * HK)], bufs[slot], sem.at[slot]).wait()

    mask = mask_ref[...]

    def compute(tok0, slot):
        buf = bufs[slot]
        xs = x_ref[pl.ds(tok0, G)]
        vbs = []
        rms = []
        for j in range(G):
            w = buf[j * HK:(j + 1) * HK].reshape(HK * R, PEER_LANES)
            ub = pltpu.bitcast(w << 16, jnp.float32).astype(jnp.bfloat16)
            vbs.append(pltpu.bitcast(w & jnp.uint32(0xFFFF0000), jnp.float32).astype(jnp.bfloat16))
            xh, xl = _split_bf16(xs[j])
            x16 = jnp.concatenate([xh, xl], axis=0)
            rt = lax.dot_general(x16, ub, (((1,), (1,)), ((), ())), preferred_element_type=jnp.float32)
            rms.append((rt[0:R] + rt[R:]) * mask)
        rm = jnp.concatenate(rms, axis=0)
        rh, rl = _split_bf16(rm)
        z = jnp.dot(jnp.concatenate([rh, rl], axis=0), sel_ref[...], preferred_element_type=jnp.float32)
        z = z[0:G * R] + z[G * R:]
        act = jnp.sum(z.reshape(G, R, HK), axis=1)
        a = jax.nn.gelu(act) * g_ref[pl.ds(tok0, G), :]
        ah, al = _split_bf16(a)
        arep = jnp.dot(jnp.concatenate([ah, al], axis=0), exp_ref[...], preferred_element_type=jnp.float32)
        arep = arep[0:G] + arep[G:]
        for j in range(G):
            am = jnp.broadcast_to(arep[j:j + 1], (R, HK * R)) * mask
            amh, aml = _split_bf16(am)
            o = jnp.dot(jnp.concatenate([amh, aml], axis=0), vbs[j], preferred_element_type=jnp.float32)
            out_ref[tok0 + j] = o[0:R] + o[R:]

    @pl.when(step == 0)
    def _():
        for a in range(PEER_AHEAD):
            issue(a * G, a)

    def body(r, c):
        for s in range(PEER_NSLOT):
            tok0 = pl.multiple_of((r * PEER_NSLOT + s) * G, G)
            wait(s)
            issue(tok0 + PEER_AHEAD * G, (s + PEER_AHEAD) % PEER_NSLOT)
            compute(tok0, s)
        return c

    lax.fori_loop(0, PEER_NG // PEER_NSLOT, body, 0)

    @pl.when(step == last)
    def _():
        for a in range(PEER_AHEAD):
            wait(a)


def _peer_experts(idx, g, x3, table):
    T = idx.shape[0]
    ns = T // PEER_TT
    HK, R = PEER_HK, PEER_ROWS
    r = np.arange(HK * R)
    sel = jnp.asarray(r[:, None] // R == np.arange(HK)[None, :], jnp.bfloat16)
    mask = jnp.asarray(r[None, :] % R == np.arange(R)[:, None], jnp.float32)
    idx3 = idx.reshape(ns, PEER_TT, HK)
    ahead = PEER_AHEAD * PEER_G
    nxt = jnp.concatenate([idx3[1:, :ahead], idx3[-1:, :ahead]], axis=0)
    idx_ext = jnp.concatenate([idx3, nxt], axis=1)
    return pl.pallas_call(
        _peer_kernel,
        grid=(ns,),
        in_specs=[
            pl.BlockSpec((1, PEER_TX, HK), lambda i: (i, 0, 0), memory_space=pltpu.SMEM),
            pl.BlockSpec((PEER_TT, HK), lambda i: (i, 0)),
            pl.BlockSpec((PEER_TT, R, PEER_LANES), lambda i: (i, 0, 0)),
            pl.BlockSpec(memory_space=pl.ANY),
            pl.BlockSpec((HK * R, HK), lambda i: (0, 0)),
            pl.BlockSpec((HK, HK * R), lambda i: (0, 0)),
            pl.BlockSpec((R, HK * R), lambda i: (0, 0)),
        ],
        out_specs=pl.BlockSpec((PEER_TT, R, PEER_LANES), lambda i: (i, 0, 0)),
        out_shape=jax.ShapeDtypeStruct((T, R, PEER_LANES), jnp.float32),
        scratch_shapes=[pltpu.VMEM((PEER_G * HK, R, PEER_LANES), jnp.uint32) for _ in range(PEER_NSLOT)]
        + [pltpu.SemaphoreType.DMA((PEER_NSLOT,))],
        compiler_params=pltpu.CompilerParams(dimension_semantics=("arbitrary",)),
        name="peer_experts",
    )(idx_ext, g, x3, table, sel, sel.T, mask)


def _pack_expert_tables(u, v):
    ub = lax.bitcast_convert_type(u.astype(jnp.bfloat16), jnp.uint16).astype(jnp.uint32)
    vb = lax.bitcast_convert_type(v.astype(jnp.bfloat16), jnp.uint16).astype(jnp.uint32)
    return ((vb << 16) | ub).reshape(u.shape[0], PEER_ROWS, PEER_LANES)


ROUTE_TT = 128
ROUTE_GROUP = 8


def _cand_layout():
    K = PEER_TOPK
    ii, jj, ok = [], [], []

    def add(i_list, j_list, ok_list):
        ii.extend(i_list)
        jj.extend(j_list)
        ok.extend(ok_list)

    add([0] * 8, list(range(8)), [True] * 8)
    add([0] * 8, list(range(8, 16)), [True] * 8)
    for i in range(1, 8):
        jmax = K // (i + 1) - 1
        add([i] * 8, list(range(8)), [j <= jmax for j in range(8)])
    add(list(range(8, 16)), [0] * 8, [True] * 8)
    return np.array(ii), np.array(jj), np.array(ok)


_CAND_I, _CAND_J, _CAND_OK = _cand_layout()
ROUTE_NC = len(_CAND_I)


def _top16(s):
    K, TT = PEER_TOPK, ROUTE_TT
    kidx = lax.broadcasted_iota(jnp.int32, (N_KEYS, TT), 0)
    sub16 = lax.broadcasted_iota(jnp.int32, (K, TT), 0)
    ninf = jnp.float32(-jnp.inf)
    ts = jnp.zeros((K, TT), jnp.float32)
    ti = jnp.zeros((K, TT), jnp.int32)
    for r in range(K):
        m = jnp.max(s, axis=0, keepdims=True)
        imin = jnp.min(jnp.where(s == m, kidx, N_KEYS), axis=0, keepdims=True)
        s = jnp.where(kidx == imin, ninf, s)
        ts = jnp.where(sub16 == r, m, ts)
        ti = jnp.where(sub16 == r, imin, ti)
    return ts, ti


def _head_route(s0, s1, flat, neg):
    K, TT = PEER_TOPK, ROUTE_TT
    sub16 = lax.broadcasted_iota(jnp.int32, (K, TT), 0)
    ninf = jnp.float32(-jnp.inf)
    ts0, ti0 = _top16(s0)
    ts1, ti1 = _top16(s1)
    e0 = ti0 * N_KEYS

    def row(a, i):
        return jnp.broadcast_to(a[i:i + 1, :], (ROUTE_GROUP, TT))

    cv = [row(ts0, 0) + ts1[0:8], row(ts0, 0) + ts1[8:16]]
    ev = [row(e0, 0) + ti1[0:8], row(e0, 0) + ti1[8:16]]
    for i in range(1, 8):
        cv.append(row(ts0, i) + ts1[0:8])
        ev.append(row(e0, i) + ti1[0:8])
    cv.append(ts0[8:16] + row(ts1, 0))
    ev.append(e0[8:16] + row(ti1, 0))
    cand = jnp.concatenate(cv, axis=0) + neg
    eid = jnp.concatenate(ev, axis=0)
    cs = jnp.zeros((K, TT), jnp.float32)
    ei = jnp.zeros((K, TT), jnp.int32)
    m0 = None
    for r in range(K):
        m = jnp.max(cand, axis=0, keepdims=True)
        if r == 0:
            m0 = m
        fmin = jnp.min(jnp.where(cand == m, flat, 4 * K * K), axis=0, keepdims=True)
        hit = flat == fmin
        e = jnp.max(jnp.where(hit, eid, -1), axis=0, keepdims=True)
        cand = jnp.where(hit, ninf, cand)
        cs = jnp.where(sub16 == r, m, cs)
        ei = jnp.where(sub16 == r, e, ei)
    p = jnp.exp(cs - m0)
    g = p / jnp.sum(p, axis=0, keepdims=True)
    return ei, g


def _route_kernel(h_ref, gn_ref, ws_ref, flat_ref, neg_ref, xn_ref, idx_ref, g_ref, s_ref, it_ref, gt_ref):
    x = h_ref[...]
    xn = x * lax.rsqrt(jnp.mean(x * x, axis=-1, keepdims=True) + EPS) * gn_ref[...]
    xn_ref[...] = xn
    s_ref[...] = lax.dot_general(ws_ref[...], xn.astype(jnp.bfloat16), (((1,), (1,)), ((), ())),
                                 preferred_element_type=jnp.float32)
    flat = flat_ref[...]
    neg = neg_ref[...]

    def head(h, c):
        base = pl.multiple_of(h * 2 * N_KEYS, 2 * N_KEYS)
        ei, g = _head_route(s_ref[pl.ds(base, N_KEYS), :], s_ref[pl.ds(base + N_KEYS, N_KEYS), :], flat, neg)
        ob = pl.multiple_of(h * PEER_TOPK, PEER_TOPK)
        it_ref[pl.ds(ob, PEER_TOPK), :] = ei
        gt_ref[pl.ds(ob, PEER_TOPK), :] = g
        return c

    lax.fori_loop(0, PEER_HEADS, head, 0)
    idx_ref[...] = it_ref[...].T
    g_ref[...] = gt_ref[...].T


def _peer_route(h2, g_ffn, ws_t):
    T = h2.shape[0]
    TT, K, NC, HK = ROUTE_TT, PEER_TOPK, ROUTE_NC, PEER_HK
    nrow = PEER_HEADS * 2 * N_KEYS
    flat = jnp.asarray(np.broadcast_to((_CAND_I * K + _CAND_J)[:, None], (NC, TT)), jnp.int32)
    neg = jnp.asarray(np.broadcast_to(np.where(_CAND_OK, 0.0, -np.inf)[:, None], (NC, TT)), jnp.float32)
    return pl.pallas_call(
        _route_kernel,
        grid=(T // TT,),
        in_specs=[
            pl.BlockSpec((TT, D_MODEL), lambda i: (i, 0)),
            pl.BlockSpec((1, D_MODEL), lambda i: (0, 0)),
            pl.BlockSpec((nrow, D_MODEL), lambda i: (0, 0)),
            pl.BlockSpec((NC, TT), lambda i: (0, 0)),
            pl.BlockSpec((NC, TT), lambda i: (0, 0)),
        ],
        out_specs=[
            pl.BlockSpec((TT, D_MODEL), lambda i: (i, 0)),
            pl.BlockSpec((TT, HK), lambda i: (i, 0)),
            pl.BlockSpec((TT, HK), lambda i: (i, 0)),
        ],
        out_shape=[
            jax.ShapeDtypeStruct((T, D_MODEL), jnp.float32),
            jax.ShapeDtypeStruct((T, HK), jnp.int32),
            jax.ShapeDtypeStruct((T, HK), jnp.float32),
        ],
        scratch_shapes=[pltpu.VMEM((nrow, TT), jnp.float32),
                        pltpu.VMEM((HK, TT), jnp.int32),
                        pltpu.VMEM((HK, TT), jnp.float32)],
        compiler_params=pltpu.CompilerParams(dimension_semantics=("arbitrary",)),
        name="peer_route",
    )(h2, g_ffn.reshape(1, D_MODEL), ws_t, flat, neg)


def _score_weights(wq, keys):
    n = PEER_HEADS * 2
    eye = jnp.eye(n, dtype=keys.dtype)
    kd = keys.reshape(n, N_KEYS, D_HALF)
    block_diag = (eye[:, None, :, None] * kd[:, :, None, :]).reshape(n * N_KEYS, n * D_HALF)
    return _matmul(block_diag, wq.T, tm=512, tn=512).astype(jnp.bfloat16)


def _peer(h2, g_ffn, wq, keys, u_tab, v_tab):
    T = h2.shape[0]
    xn, experts, g = _peer_route(h2, g_ffn, _score_weights(wq, keys))
    out = _peer_experts(experts, g, xn.reshape(T, PEER_ROWS, PEER_LANES), _pack_expert_tables(u_tab, v_tab))
    return out.reshape(T, D_MODEL)


def kernel(x, meta, g_final, g_mix, w_in, b_gate, hy_conv_w, hy_conv_b, hy_w1, hy_b1, hy_w2, hy_b2,
           hy_w3, hy_b3, hy_freq, hy_skip, hy_out, rg_conv_w, rg_conv_b, rg_gate_w, rg_gate_b,
           rg_lambda, rg_out, w_out, g_ffn, peer_wq, peer_keys, peer_u, peer_v):
    B = x.shape[0]
    h = jnp.concatenate([jnp.broadcast_to(meta[None], (B, N_META, D_MODEL)), x], axis=1)
    L = h.shape[1]
    s_hy = 3 * D_HY
    s_rx = s_hy + D_RG
    s_rg = s_rx + D_RG
    l = 0
    h0 = h.reshape(B * L, D_MODEL)
    proj2 = _norm_matmul(h0, g_mix[l], w_in[l])
    proj = proj2.reshape(B, L, -1)
    y_hy = _hyena_branch(proj[..., :s_hy], hy_conv_w[l], hy_conv_b[l], hy_w1[l], hy_b1[l], hy_w2[l],
                         hy_b2[l], hy_w3[l], hy_b3[l], hy_freq[l], hy_skip[l], hy_out[l])
    rg_h = _rglru_states(proj[..., s_hy:s_rx], rg_conv_w[l], rg_conv_b[l], rg_gate_w[l], rg_gate_b[l],
                         rg_lambda[l])
    y_rg = _gelu_gate_matmul(rg_h, proj2, s_rx, rg_out[l])
    h2 = _merge_matmul(proj2, s_rg, b_gate[l], y_hy.reshape(B * L, D_MODEL), y_rg, w_out[l], h0)
    p = _peer(h2, g_ffn[l], peer_wq[l], peer_keys[l], peer_u[l], peer_v[l])
    out = _add_norm(h2, p, g_final).reshape(B, L, D_MODEL)
    return out[:, N_META:]
```
